```python
import math
import jax, jax.numpy as jnp
from jax import lax
import numpy as np

D_MODEL = 2048
BATCH = 1
SEQ = 16384
DEPTH = 2
DEC_BATCH = 4
DEC_SEQ = 4096
PAST_LEN = 128

MIX_WIDTH = D_MODEL
N_ATTN_HEADS = 8
ATTN_HEAD_DIM = 64
ATTN_WIDTH = N_ATTN_HEADS * 2 * ATTN_HEAD_DIM
SSM_WIDTH = MIX_WIDTH - ATTN_WIDTH
SSM_HEAD_DIM = 64
N_SSM_HEADS = SSM_WIDTH // SSM_HEAD_DIM
N_SSM_GROUPS = 4
SSM_HPG = N_SSM_HEADS // N_SSM_GROUPS
D_STATE = 128
D_CONV = 5
CHUNK = 128
CONV_CH = SSM_WIDTH + 2 * N_SSM_GROUPS * D_STATE
OFF_Q = 0
OFF_K = OFF_Q + ATTN_WIDTH
OFF_V = OFF_K + ATTN_WIDTH
OFF_Z = OFF_V + ATTN_WIDTH
OFF_XBC = OFF_Z + SSM_WIDTH
OFF_DT = OFF_XBC + CONV_CH
IN_W = OFF_DT + 2 * N_SSM_HEADS
D_FF = 4 * D_MODEL
NUM_BUCKETS = 32
MAX_DISTANCE = 128
Q_BLOCK = 128
EPS = 1e-6

kernel_name = "hybrid_diffattn_ssd_encoder"


def rms_norm(x, w):
    xf = x.astype(jnp.float32)
    y = xf * lax.rsqrt(jnp.mean(xf * xf, axis=-1, keepdims=True) + EPS)
    return (y * w.astype(jnp.float32)).astype(x.dtype)


def rel_bucket(rel):
    nb = NUM_BUCKETS // 2
    max_exact = nb // 2
    ret = jnp.where(rel > 0, nb, 0)
    n = jnp.abs(rel)
    nf = jnp.maximum(n, 1).astype(jnp.float32)
    large = max_exact + (jnp.log(nf / max_exact) / math.log(MAX_DISTANCE / max_exact)
                         * (nb - max_exact)).astype(jnp.int32)
    large = jnp.minimum(large, nb - 1)
    return ret + jnp.where(n < max_exact, n, large)


def diff_attention(q, k, v, lam, lambda_init, subln_w, rel_table):
    bsz, seq = q.shape[0], q.shape[1]
    nq = seq // Q_BLOCK
    qb = (q * (ATTN_HEAD_DIM ** -0.5)).reshape(bsz, nq, Q_BLOCK, N_ATTN_HEADS, 2, ATTN_HEAD_DIM)
    qb = jnp.moveaxis(qb, 1, 0)
    k_pos = jnp.arange(seq)

    def block(args):
        q_blk, i = args
        logits = jnp.einsum("bqhcd,bkhcd->bchqk", q_blk, k).astype(jnp.float32)
        q_pos = i * Q_BLOCK + jnp.arange(Q_BLOCK)
        bucket = rel_bucket(k_pos[None, :] - q_pos[:, None])
        bias = jnp.transpose(rel_table[bucket], (2, 0, 1)).astype(jnp.float32)
        p = jax.nn.softmax(logits + bias[None, None], axis=-1)
        w = p[:, 0] - lam * p[:, 1]
        return jnp.einsum("bhqk,bkhe->bqhe", w.astype(v.dtype), v)

    out = lax.map(block, (qb, jnp.arange(nq)))
    out = jnp.moveaxis(out, 0, 1).reshape(bsz, seq, N_ATTN_HEADS, 2 * ATTN_HEAD_DIM)
    out = rms_norm(out, subln_w) * (1.0 - lambda_init)
    return out.reshape(bsz, seq, ATTN_WIDTH)


def depthwise_conv(x, w, b):
    y = lax.conv_general_dilated(x, w[:, None, :].astype(x.dtype), window_strides=(1,),
                                 padding=[(D_CONV // 2, D_CONV // 2)],
                                 dimension_numbers=("NWC", "WIO", "NWC"),
                                 feature_group_count=x.shape[-1])
    return y + b.astype(x.dtype)


def segsum_exp(a):
    cs = jnp.cumsum(a, axis=-1)
    diff = cs[..., :, None] - cs[..., None, :]
    mask = jnp.tril(jnp.ones((a.shape[-1], a.shape[-1]), dtype=bool))
    return jnp.exp(jnp.where(mask, diff, -jnp.inf))


def ssd_scan(x, dt, A, Bm, Cm):
    bsz, seq = x.shape[0], x.shape[1]
    nc = seq // CHUNK
    f32 = jnp.float32
    xc = x.astype(f32).reshape(bsz, nc, CHUNK, N_SSM_GROUPS, SSM_HPG, SSM_HEAD_DIM)
    Bc = Bm.astype(f32).reshape(bsz, nc, CHUNK, N_SSM_GROUPS, D_STATE)
    Cc = Cm.astype(f32).reshape(bsz, nc, CHUNK, N_SSM_GROUPS, D_STATE)
    dtc = dt.reshape(bsz, nc, CHUNK, N_SSM_GROUPS, SSM_HPG)
    a = jnp.moveaxis(dtc * A, 2, -1)
    a_cs = jnp.cumsum(a, axis=-1)
    xdt = xc * dtc[..., None]
    decay = segsum_exp(a)
    cb = jnp.einsum("bclgn,bcsgn->bcgls", Cc, Bc)
    y_diag = jnp.einsum("bcgls,bcghls,bcsghp->bclghp", cb, decay, xdt)
    decay_to_end = jnp.exp(a_cs[..., -1:] - a_cs)
    states = jnp.einsum("bcsgn,bcghs,bcsghp->bcghpn", Bc, decay_to_end, xdt)
    chunk_decay = jnp.exp(a_cs[..., -1])

    def step(h, inp):
        s, d = inp
        return h * d[..., None, None] + s, h

    h0 = jnp.zeros((bsz, N_SSM_GROUPS, SSM_HPG, SSM_HEAD_DIM, D_STATE), f32)
    _, prev = lax.scan(step, h0, (jnp.moveaxis(states, 1, 0), jnp.moveaxis(chunk_decay, 1, 0)))
    prev = jnp.moveaxis(prev, 0, 1)
    y_off = jnp.einsum("bclgn,bcghpn,bcghl->bclghp", Cc, prev, jnp.exp(a_cs))
    return (y_diag + y_off).reshape(bsz, seq, N_SSM_GROUPS, SSM_HPG, SSM_HEAD_DIM)


def ssd_mixer(z, xbc, dt_raw, conv_w, conv_b, dt_bias_f, dt_bias_b, a_log_f, a_log_b, d_skip, norm_w):
    bsz, seq = z.shape[0], z.shape[1]
    f32 = jnp.float32
    xbc = jax.nn.silu(depthwise_conv(xbc, conv_w, conv_b))
    xs = xbc[..., :SSM_WIDTH].reshape(bsz, seq, N_SSM_GROUPS, SSM_HPG, SSM_HEAD_DIM)
    Bm = xbc[..., SSM_WIDTH:SSM_WIDTH + N_SSM_GROUPS * D_STATE].reshape(bsz, seq, N_SSM_GROUPS, D_STATE)
    Cm = xbc[..., SSM_WIDTH + N_SSM_GROUPS * D_STATE:].reshape(bsz, seq, N_SSM_GROUPS, D_STATE)
    dt_raw = dt_raw.astype(f32)
    dt_f = jax.nn.softplus(dt_raw[..., :N_SSM_HEADS] + dt_bias_f.astype(f32)).reshape(bsz, seq, N_SSM_GROUPS, SSM_HPG)
    dt_b = jax.nn.softplus(dt_raw[..., N_SSM_HEADS:] + dt_bias_b.astype(f32)).reshape(bsz, seq, N_SSM_GROUPS, SSM_HPG)
    A_f = -jnp.exp(a_log_f.astype(f32)).reshape(N_SSM_GROUPS, SSM_HPG)
    A_b = -jnp.exp(a_log_b.astype(f32)).reshape(N_SSM_GROUPS, SSM_HPG)
    y_f = ssd_scan(xs, dt_f, A_f, Bm, Cm)
    flip = lambda t: jnp.flip(t, axis=1)
    y_b = flip(ssd_scan(flip(xs), flip(dt_b), A_b, flip(Bm), flip(Cm)))
    y = y_f + y_b + d_skip.astype(f32).reshape(N_SSM_GROUPS, SSM_HPG)[..., None] * xs.astype(f32)
    y = y.reshape(bsz, seq, SSM_WIDTH) * jax.nn.silu(z.astype(f32))
    y = rms_norm(y.reshape(bsz, seq, N_SSM_GROUPS, SSM_WIDTH // N_SSM_GROUPS),
                 norm_w.reshape(N_SSM_GROUPS, SSM_WIDTH // N_SSM_GROUPS))
    return y.reshape(bsz, seq, SSM_WIDTH).astype(z.dtype)


def trunk(x, params):
    (rel_bias, pre_norm_mix, w_in, lambda_q1, lambda_k1, lambda_q2, lambda_k2, attn_norm,
     conv_w, conv_b, dt_bias_fwd, dt_bias_bwd, a_log_fwd, a_log_bwd, d_skip, ssm_norm,
     w_out, post_norm_mix, pre_norm_mlp, w_up, w_down, post_norm_mlp) = params
    bsz, seq = x.shape[0], x.shape[1]
    for i in range(DEPTH):
        lambda_init = 0.8 - 0.6 * math.exp(-0.3 * i)
        h = rms_norm(x, pre_norm_mix[i])
        p = h @ w_in[i]
        q = p[..., OFF_Q:OFF_K].reshape(bsz, seq, N_ATTN_HEADS, 2, ATTN_HEAD_DIM)
        k = p[..., OFF_K:OFF_V].reshape(bsz, seq, N_ATTN_HEADS, 2, ATTN_HEAD_DIM)
        v = p[..., OFF_V:OFF_Z].reshape(bsz, seq, N_ATTN_HEADS, 2 * ATTN_HEAD_DIM)
        lam = (jnp.exp(jnp.sum(lambda_q1[i].astype(jnp.float32) * lambda_k1[i].astype(jnp.float32)))
               - jnp.exp(jnp.sum(lambda_q2[i].astype(jnp.float32) * lambda_k2[i].astype(jnp.float32)))
               + lambda_init)
        a_out = diff_attention(q, k, v, lam, lambda_init, attn_norm[i], rel_bias)
        s_out = ssd_mixer(p[..., OFF_Z:OFF_XBC], p[..., OFF_XBC:OFF_DT], p[..., OFF_DT:],
                          conv_w[i], conv_b[i], dt_bias_fwd[i], dt_bias_bwd[i],
                          a_log_fwd[i], a_log_bwd[i], d_skip[i], ssm_norm[i])
        mix = jnp.concatenate([a_out, s_out], axis=-1) @ w_out[i]
        x = x + rms_norm(mix, post_norm_mix[i])
        h = rms_norm(x, pre_norm_mlp[i])
        u = jnp.square(jax.nn.relu(h @ w_up[i]))
        x = x + rms_norm(u @ w_down[i], post_norm_mlp[i])
    return x


def setup_inputs(seed: int = 0) -> dict:
    key = jax.random.key(seed)
    ks = jax.random.split(key, 24)
    f32 = jnp.float32

    def nrm(k, shape, scale):
        return jax.random.normal(k, shape, f32) * scale

    def gain(k, shape):
        return 1.0 + 0.01 * jax.random.normal(k, shape, f32)

    dt = jnp.exp(jax.random.uniform(ks[12], (2, DEPTH, N_SSM_HEADS), f32, math.log(1e-3), math.log(1e-1)))
    dt_bias = dt + jnp.log(-jnp.expm1(-dt))
    a_log = jnp.log(jax.random.uniform(ks[13], (2, DEPTH, N_SSM_HEADS), f32, 1.0, 16.0))
    return {
        "x_prompt": nrm(ks[0], (BATCH, SEQ, D_MODEL), 1.0),
        "x_sample": nrm(ks[1], (DEC_BATCH, DEC_SEQ, D_MODEL), 1.0),
        "rel_bias": nrm(ks[2], (NUM_BUCKETS, N_ATTN_HEADS), 0.5),
        "pre_norm_mix": gain(ks[3], (DEPTH, D_MODEL)),
        "w_in": nrm(ks[4], (DEPTH, D_MODEL, IN_W), D_MODEL ** -0.5),
        "lambda_q1": nrm(ks[5], (DEPTH, ATTN_HEAD_DIM), 0.1),
        "lambda_k1": nrm(ks[6], (DEPTH, ATTN_HEAD_DIM), 0.1),
        "lambda_q2": nrm(ks[7], (DEPTH, ATTN_HEAD_DIM), 0.1),
        "lambda_k2": nrm(ks[8], (DEPTH, ATTN_HEAD_DIM), 0.1),
        "attn_norm": gain(ks[9], (DEPTH, 2 * ATTN_HEAD_DIM)),
        "conv_w": nrm(ks[10], (DEPTH, D_CONV, CONV_CH), D_CONV ** -0.5),
        "conv_b": nrm(ks[11], (DEPTH, CONV_CH), 0.01),
        "dt_bias_fwd": dt_bias[0],
        "dt_bias_bwd": dt_bias[1],
        "a_log_fwd": a_log[0],
        "a_log_bwd": a_log[1],
        "d_skip": gain(ks[14], (DEPTH, N_SSM_HEADS)),
        "ssm_norm": gain(ks[15], (DEPTH, SSM_WIDTH)),
        "w_out": nrm(ks[16], (DEPTH, MIX_WIDTH, D_MODEL), MIX_WIDTH ** -0.5),
        "post_norm_mix": gain(ks[17], (DEPTH, D_MODEL)),
        "pre_norm_mlp": gain(ks[18], (DEPTH, D_MODEL)),
        "w_up": nrm(ks[19], (DEPTH, D_MODEL, D_FF), D_MODEL ** -0.5),
        "w_down": nrm(ks[20], (DEPTH, D_FF, D_MODEL), D_FF ** -0.5),
        "post_norm_mlp": gain(ks[21], (DEPTH, D_MODEL)),
    }


def reference(x_prompt, x_sample, rel_bias, pre_norm_mix, w_in, lambda_q1, lambda_k1, lambda_q2,
              lambda_k2, attn_norm, conv_w, conv_b, dt_bias_fwd, dt_bias_bwd, a_log_fwd, a_log_bwd,
              d_skip, ssm_norm, w_out, post_norm_mix, pre_norm_mlp, w_up, w_down, post_norm_mlp):
    params = (rel_bias, pre_norm_mix, w_in, lambda_q1, lambda_k1, lambda_q2, lambda_k2, attn_norm,
              conv_w, conv_b, dt_bias_fwd, dt_bias_bwd, a_log_fwd, a_log_bwd, d_skip, ssm_norm,
              w_out, post_norm_mix, pre_norm_mlp, w_up, w_down, post_norm_mlp)
    y_prompt = trunk(x_prompt, params)
    y_sample = trunk(x_sample, params)
    return (y_prompt, y_sample)
```

```python
import functools
import math

import jax
import jax.numpy as jnp
import numpy as np
from jax import lax
from jax.experimental import pallas as pl
from jax.experimental.pallas import tpu as pltpu

f32 = jnp.float32
bf16 = jnp.bfloat16

D_MODEL = 2048
N_ATTN_HEADS = 8
ATTN_HEAD_DIM = 64
ATTN_V_DIM = 2 * ATTN_HEAD_DIM
ATTN_WIDTH = N_ATTN_HEADS * ATTN_V_DIM
SSM_WIDTH = 1024
SSM_HEAD_DIM = 64
N_SSM_HEADS = 16
N_SSM_GROUPS = 4
SSM_HPG = 4
D_STATE = 128
D_CONV = 5
CHUNK = 128
CONV_CH = SSM_WIDTH + 2 * N_SSM_GROUPS * D_STATE
GROUP_W = SSM_HPG * SSM_HEAD_DIM
OFF_Q, OFF_K, OFF_V, OFF_Z = 0, 1024, 2048, 3072
OFF_XBC = 4096
OFF_DT = OFF_XBC + CONV_CH
D_FF = 4 * D_MODEL
NUM_BUCKETS = 32
MAX_DISTANCE = 128
EPS = 1e-6
NEG = -1e30

LANES = 128
HALO = 16
VMEM_LIMIT = 56 * 1024 * 1024


def _cparams(n_axes):
    return pltpu.CompilerParams(dimension_semantics=("arbitrary",) * n_axes,
                                vmem_limit_bytes=VMEM_LIMIT)


def _rms(xf, g):
    ms = jnp.mean(xf * xf, axis=-1, keepdims=True)
    return (xf * lax.rsqrt(ms + EPS)) * g


def _split3(v):
    hi = v.astype(bf16)
    r1 = v - hi.astype(f32)
    mid = r1.astype(bf16)
    lo = (r1 - mid.astype(f32)).astype(bf16)
    return hi, mid, lo


def _dot3(v, m01):
    hi, mid, lo = _split3(v)
    d = lambda a: jnp.dot(a, m01, preferred_element_type=f32)
    return d(hi) + d(mid) + d(lo)


def _dot3_left(m01, v):
    hi, mid, lo = _split3(v)
    d = lambda a: jnp.dot(m01, a, preferred_element_type=f32)
    return d(hi) + d(mid) + d(lo)


def _in_proj_kernel(x_ref, g_ref, w_ref, wdt_ref, qk_ref, vt_ref, xz_ref, dt_ref, h_ref):
    j = pl.program_id(1)

    @pl.when(j == 0)
    def _():
        hb = _rms(x_ref[...], g_ref[...]).astype(bf16)
        h_ref[...] = hb
        dt_ref[...] = jnp.dot(hb, wdt_ref[...], preferred_element_type=f32)

    acc = jnp.dot(h_ref[...], w_ref[...], preferred_element_type=f32)

    @pl.when(j == 0)
    def _():
        qk_ref[...] = (acc * (ATTN_HEAD_DIM ** -0.5)).astype(bf16)

    @pl.when(j == 1)
    def _():
        qk_ref[...] = acc.astype(bf16)

    @pl.when(j == 2)
    def _():
        tm = acc.shape[0]
        vt_ref[0, :, 0] = acc.T.astype(bf16).reshape(N_ATTN_HEADS, ATTN_V_DIM, tm)

    @pl.when(j >= 3)
    def _():
        xz_ref[...] = acc.astype(bf16)


def _in_proj(x2, gain, w_main, w_dt, bsz, seq, tm):
    rows = x2.shape[0]
    nl = seq // tm
    nj = w_main.shape[1] // ATTN_WIDTH
    return pl.pallas_call(
        _in_proj_kernel,
        grid=(rows // tm, nj),
        in_specs=[
            pl.BlockSpec((tm, D_MODEL), lambda i, j: (i, 0)),
            pl.BlockSpec((1, D_MODEL), lambda i, j: (0, 0)),
            pl.BlockSpec((D_MODEL, ATTN_WIDTH), lambda i, j: (0, j)),
            pl.BlockSpec((D_MODEL, LANES), lambda i, j: (0, 0)),
        ],
        out_specs=[
            pl.BlockSpec((tm, ATTN_WIDTH), lambda i, j: (i, jnp.minimum(j, 1))),
            pl.BlockSpec((1, N_ATTN_HEADS, 1, ATTN_V_DIM, tm), lambda i, j: (i // nl, 0, i % nl, 0, 0)),
            pl.BlockSpec((tm, ATTN_WIDTH), lambda i, j: (i, jnp.clip(j - 3, 0, 2))),
            pl.BlockSpec((tm, LANES), lambda i, j: (i, 0)),
        ],
        out_shape=[
            jax.ShapeDtypeStruct((rows, 2 * ATTN_WIDTH), bf16),
            jax.ShapeDtypeStruct((bsz, N_ATTN_HEADS, nl, ATTN_V_DIM, tm), bf16),
            jax.ShapeDtypeStruct((rows, CONV_CH + SSM_WIDTH), bf16),
            jax.ShapeDtypeStruct((rows, LANES), f32),
        ],
        scratch_shapes=[pltpu.VMEM((tm, D_MODEL), bf16)],
        compiler_params=_cparams(2),
        name="in_proj",
    )(x2, gain, w_main, w_dt)


def _bias_kernel(tab_ref, o_ref, *, t):
    h = pl.program_id(0)
    d = pl.program_id(1) - 1
    kk = lax.broadcasted_iota(jnp.int32, (t, t), 0)
    qq = lax.broadcasted_iota(jnp.int32, (t, t), 1)
    rel = d * t + kk - qq
    nb = NUM_BUCKETS // 2
    max_exact = nb // 2
    ret = jnp.where(rel > 0, nb, 0)
    n = jnp.abs(rel)
    nf = jnp.maximum(n, 1).astype(f32)
    large = max_exact + (jnp.log(nf / max_exact) / math.log(MAX_DISTANCE / max_exact)
                         * (nb - max_exact)).astype(jnp.int32)
    large = jnp.minimum(large, nb - 1)
    bucket = ret + jnp.where(n < max_exact, n, large)
    out = jnp.zeros((t, t), f32)
    for b in range(NUM_BUCKETS):
        out = jnp.where(bucket == b, tab_ref[b, h], out)
    o_ref[0, 0] = out


def _bias_tiles(rel_bias, t):
    return pl.pallas_call(
        functools.partial(_bias_kernel, t=t),
        grid=(N_ATTN_HEADS, 3),
        in_specs=[pl.BlockSpec(memory_space=pltpu.SMEM)],
        out_specs=pl.BlockSpec((1, 1, t, t), lambda h, d: (h, d, 0, 0)),
        out_shape=jax.ShapeDtypeStruct((N_ATTN_HEADS, 3, t, t), f32),
        compiler_params=_cparams(2),
        name="bias_tiles",
    )(rel_bias)


def _attn_kernel(tab_ref, lam_ref, q_ref, k_ref, vt_ref, bias_ref, w_ref, o_ref, *, t, nk, out_scale):
    h = pl.program_id(1)
    qi = pl.program_id(2)
    c_left = tab_ref[NUM_BUCKETS // 2 - 1, h]
    c_right = tab_ref[NUM_BUCKETS - 1, h]
    lam = lam_ref[0]

    qt = q_ref[0, 0].astype(f32).T
    row = lax.broadcasted_iota(jnp.int32, qt.shape, 0)
    qcat = jnp.concatenate([jnp.where(row < ATTN_HEAD_DIM, qt, 0.0),
                            jnp.where(row >= ATTN_HEAD_DIM, qt, 0.0)], axis=1).astype(bf16)
    ones = jnp.ones((8, t), bf16)

    def step(kt, carry, bias, c):
        m, acc = carry
        s = jnp.dot(k_ref[0, kt], qcat, preferred_element_type=f32)
        if bias is not None:
            s = s + bias
        m_new = jnp.maximum(m, jnp.max(s, axis=0, keepdims=True) + c)
        alpha = jnp.exp(m - m_new)
        p = jnp.exp(s - (m_new - c)).astype(bf16)
        vext = jnp.concatenate([vt_ref[0, 0, kt], ones], axis=0)
        acc = acc * alpha + jnp.dot(vext, p, preferred_element_type=f32)
        return m_new, acc

    def near(kt, carry):
        b = bias_ref[0, kt - qi + 1]
        return step(kt, carry, jnp.concatenate([b, b], axis=1), 0.0)

    carry = (jnp.full((1, 2 * t), NEG, f32), jnp.zeros((ATTN_V_DIM + 8, 2 * t), f32))
    near_lo = jnp.maximum(qi - 1, 0)
    near_hi = jnp.minimum(qi + 2, nk)
    carry = lax.fori_loop(0, near_lo, lambda kt, cr: step(kt, cr, None, c_left), carry)
    carry = lax.fori_loop(near_lo, near_hi, near, carry)
    carry = lax.fori_loop(near_hi, nk, lambda kt, cr: step(kt, cr, None, c_right), carry)

    _, acc = carry
    o = acc[:ATTN_V_DIM] * (1.0 / acc[ATTN_V_DIM:ATTN_V_DIM + 1])
    diff = (o[:, :t] - lam * o[:, t:]).T
    o_ref[0] = (_rms(diff, w_ref[...]) * out_scale).astype(bf16)


def _attention(qk4, vt, bias, rel_bias, lam, subln_w, lambda_init, t):
    bsz, nk = qk4.shape[0], qk4.shape[1]
    kern = functools.partial(_attn_kernel, t=t, nk=nk, out_scale=1.0 - lambda_init)
    return pl.pallas_call(
        kern,
        grid=(bsz, N_ATTN_HEADS, nk),
        in_specs=[
            pl.BlockSpec(memory_space=pltpu.SMEM),
            pl.BlockSpec(memory_space=pltpu.SMEM),
            pl.BlockSpec((1, 1, t, ATTN_V_DIM), lambda b, h, i: (b, i, 0, h)),
            pl.BlockSpec((1, nk, t, ATTN_V_DIM), lambda b, h, i: (b, 0, 0, N_ATTN_HEADS + h)),
            pl.BlockSpec((1, 1, nk, ATTN_V_DIM, t), lambda b, h, i: (b, h, 0, 0, 0)),
            pl.BlockSpec((1, 3, t, t), lambda b, h, i: (h, 0, 0, 0)),
            pl.BlockSpec((1, ATTN_V_DIM), lambda b, h, i: (0, 0)),
        ],
        out_specs=pl.BlockSpec((1, t, ATTN_V_DIM), lambda b, h, i: (b, i, h)),
        out_shape=jax.ShapeDtypeStruct((bsz, nk * t, ATTN_WIDTH), bf16),
        compiler_params=_cparams(3),
        name="attention",
    )(rel_bias, lam, qk4, qk4, vt, bias, subln_w)


def _softplus(v):
    return jnp.maximum(v, 0.0) + jnp.log1p(jnp.exp(-jnp.abs(v)))


def _ssd_decay_terms(dt_ref, dtb_ref, alog_ref, tril_ref):
    dtv = _softplus(dt_ref[0] + dtb_ref[...])
    lane = lax.broadcasted_iota(jnp.int32, (1, LANES), 1)
    a_row = jnp.where(lane < 2 * N_SSM_HEADS, -jnp.exp(alog_ref[...]), 0.0)
    a = dtv * a_row
    cs = _dot3_left(tril_ref[...], a)
    return dtv, a, cs


def _ssd_fwd_kernel(cur_ref, prev_ref, next_ref, dt_ref, cw_ref, cb_ref, dtb_ref, alog_ref, dskip_ref,
                    tril_ref, triu_ref, ef_ref, y_ref, act_ref, state_ref):
    c = pl.program_id(1)
    nc = pl.num_programs(1)

    @pl.when(c == 0)
    def _():
        state_ref[...] = jnp.zeros_like(state_ref)

    pmask = (c > 0).astype(f32)
    nmask = (c < nc - 1).astype(f32)
    ext = jnp.concatenate([prev_ref[0, HALO - 8:, :].astype(f32) * pmask,
                           cur_ref[0].astype(f32),
                           next_ref[0, :8, :].astype(f32) * nmask], axis=0)
    conv = jnp.zeros((CHUNK, CONV_CH), f32) + cb_ref[...]
    for j in range(D_CONV):
        off = 8 + j - D_CONV // 2
        conv = conv + ext[off:off + CHUNK, :] * cw_ref[j:j + 1, :]
    act = conv * (1.0 / (1.0 + jnp.exp(-conv)))
    actb = act.astype(bf16)
    act_ref[0] = actb

    dtv, a, cs = _ssd_decay_terms(dt_ref, dtb_ref, alog_ref, tril_ref)
    a_t = a.T
    dt_t = dtv.T
    cs_t = _dot3(a_t, triu_ref[...])
    ecs = cs - a
    ecs_t = cs_t - a_t

    tot = cs[CHUNK - 1:CHUNK, :]
    scale_off = _dot3(jnp.exp(cs), ef_ref[...])
    scale_in = _dot3(dtv * jnp.exp(tot - cs), ef_ref[...])
    carry_dec = scale_off[CHUNK - 1:CHUNK, :]

    li = lax.broadcasted_iota(jnp.int32, (CHUNK, CHUNK), 0)
    si = lax.broadcasted_iota(jnp.int32, (CHUNK, CHUNK), 1)
    lower = li >= si
    upper = si >= li

    for g in range(N_SSM_GROUPS):
        gs = slice(g * GROUP_W, (g + 1) * GROUP_W)
        bg = actb[:, SSM_WIDTH + g * D_STATE: SSM_WIDTH + (g + 1) * D_STATE]
        cg = actb[:, SSM_WIDTH + N_SSM_GROUPS * D_STATE + g * D_STATE:
                  SSM_WIDTH + N_SSM_GROUPS * D_STATE + (g + 1) * D_STATE]
        cbm = lax.dot_general(cg, bg, (((1,), (1,)), ((), ())), preferred_element_type=f32)
        ws, xs_blocks = [], []
        for hh in range(SSM_HPG):
            hd = g * SSM_HPG + hh
            hb = N_SSM_HEADS + hd
            dec_f = jnp.exp(jnp.where(lower, cs[:, hd:hd + 1] - cs_t[hd:hd + 1, :], NEG))
            dec_b = jnp.exp(jnp.where(upper, ecs_t[hb:hb + 1, :] - ecs[:, hb:hb + 1], NEG))
            w = cbm * (dec_f * dt_t[hd:hd + 1, :] + dec_b * dt_t[hb:hb + 1, :])
            ws.append(w.astype(bf16))
            lane = lax.broadcasted_iota(jnp.int32, (CHUNK, GROUP_W), 1)
            own = (lane >= hh * SSM_HEAD_DIM) & (lane < (hh + 1) * SSM_HEAD_DIM)
            xs_blocks.append(jnp.where(own, actb[:, gs], jnp.zeros((), bf16)))
        xg = act[:, gs]
        wcat = jnp.concatenate(ws, axis=1)
        xblk = jnp.concatenate(xs_blocks, axis=0)
        y = jnp.dot(wcat, xblk, preferred_element_type=f32)
        y = y + jnp.dot(cg, state_ref[g].astype(bf16), preferred_element_type=f32) * scale_off[:, gs]
        y_ref[0, :, gs] = y + xg * dskip_ref[:, gs]
        xs = (xg * scale_in[:, gs]).astype(bf16)
        new = jnp.dot(bg.astype(f32).T.astype(bf16), xs, preferred_element_type=f32)
        state_ref[g] = state_ref[g] * carry_dec[:, gs] + new


def _ssd_bwd_kernel(act_ref, dt_ref, y_ref, z_ref, dtb_ref, alog_ref, nw_ref, tril_ref, eb_ref,
                    o_ref, state_ref):
    c = pl.program_id(1)

    @pl.when(c == 0)
    def _():
        state_ref[...] = jnp.zeros_like(state_ref)

    actb = act_ref[0]
    dtv, a, cs = _ssd_decay_terms(dt_ref, dtb_ref, alog_ref, tril_ref)
    ecs = cs - a
    tot = cs[CHUNK - 1:CHUNK, :]
    scale_off = _dot3(jnp.exp(tot - ecs), eb_ref[...])
    scale_in = _dot3(dtv * jnp.exp(ecs), eb_ref[...])
    carry_dec = scale_off[0:1, :]

    zf = z_ref[0].astype(f32)
    gate = zf * (1.0 / (1.0 + jnp.exp(-zf)))
    for g in range(N_SSM_GROUPS):
        gs = slice(g * GROUP_W, (g + 1) * GROUP_W)
        bg = actb[:, SSM_WIDTH + g * D_STATE: SSM_WIDTH + (g + 1) * D_STATE]
        cg = actb[:, SSM_WIDTH + N_SSM_GROUPS * D_STATE + g * D_STATE:
                  SSM_WIDTH + N_SSM_GROUPS * D_STATE + (g + 1) * D_STATE]
        y = y_ref[0, :, gs] + (jnp.dot(cg, state_ref[g].astype(bf16), preferred_element_type=f32)
                               * scale_off[:, gs])
        xs = (actb[:, gs].astype(f32) * scale_in[:, gs]).astype(bf16)
        new = jnp.dot(bg.astype(f32).T.astype(bf16), xs, preferred_element_type=f32)
        state_ref[g] = state_ref[g] * carry_dec[:, gs] + new
        o_ref[0, :, gs] = _rms(y * gate[:, gs], nw_ref[:, gs]).astype(bf16)


def _const_spec(shape):
    return pl.BlockSpec(shape, lambda b, c: (0,) * len(shape))


def _ssd_fwd(xz3, dt3, conv_w, conv_b, dt_bias, a_log, d_skip, consts):
    bsz, seq = xz3.shape[0], xz3.shape[1]
    nc = seq // CHUNK
    hb = CHUNK // HALO
    nh = seq // HALO
    tril, triu, ef, _ = consts
    const = _const_spec
    return pl.pallas_call(
        _ssd_fwd_kernel,
        grid=(bsz, nc),
        in_specs=[
            pl.BlockSpec((1, CHUNK, CONV_CH), lambda b, c: (b, c, 0)),
            pl.BlockSpec((1, HALO, CONV_CH), lambda b, c: (b, jnp.maximum(c * hb - 1, 0), 0)),
            pl.BlockSpec((1, HALO, CONV_CH), lambda b, c: (b, jnp.minimum((c + 1) * hb, nh - 1), 0)),
            pl.BlockSpec((1, CHUNK, LANES), lambda b, c: (b, c, 0)),
            const((8, CONV_CH)), const((1, CONV_CH)), const((1, LANES)), const((1, LANES)),
            const((1, SSM_WIDTH)), const((CHUNK, CHUNK)), const((CHUNK, CHUNK)), const((LANES, SSM_WIDTH)),
        ],
        out_specs=[
            pl.BlockSpec((1, CHUNK, SSM_WIDTH), lambda b, c: (b, c, 0)),
            pl.BlockSpec((1, CHUNK, CONV_CH), lambda b, c: (b, c, 0)),
        ],
        out_shape=[
            jax.ShapeDtypeStruct((bsz, seq, SSM_WIDTH), f32),
            jax.ShapeDtypeStruct((bsz, seq, CONV_CH), bf16),
        ],
        scratch_shapes=[pltpu.VMEM((N_SSM_GROUPS, D_STATE, GROUP_W), f32)],
        compiler_params=_cparams(2),
        name="ssd_fwd",
    )(xz3, xz3, xz3, dt3, conv_w, conv_b, dt_bias, a_log, d_skip, tril, triu, ef)


def _ssd_bwd(act, dt3, y_part, xz3, dt_bias, a_log, norm_w, consts):
    bsz, seq = xz3.shape[0], xz3.shape[1]
    nc = seq // CHUNK
    tril, _, _, eb = consts
    const = _const_spec
    rev = lambda b, c: (b, nc - 1 - c, 0)
    return pl.pallas_call(
        _ssd_bwd_kernel,
        grid=(bsz, nc),
        in_specs=[
            pl.BlockSpec((1, CHUNK, CONV_CH), rev),
            pl.BlockSpec((1, CHUNK, LANES), rev),
            pl.BlockSpec((1, CHUNK, SSM_WIDTH), rev),
            pl.BlockSpec((1, CHUNK, SSM_WIDTH), lambda b, c: (b, nc - 1 - c, CONV_CH // SSM_WIDTH)),
            const((1, LANES)), const((1, LANES)), const((1, SSM_WIDTH)),
            const((CHUNK, CHUNK)), const((LANES, SSM_WIDTH)),
        ],
        out_specs=pl.BlockSpec((1, CHUNK, SSM_WIDTH), rev),
        out_shape=jax.ShapeDtypeStruct((bsz, seq, SSM_WIDTH), bf16),
        scratch_shapes=[pltpu.VMEM((N_SSM_GROUPS, D_STATE, GROUP_W), f32)],
        compiler_params=_cparams(2),
        name="ssd_bwd",
    )(act, dt3, y_part, xz3, dt_bias, a_log, norm_w, tril, eb)


def _ssd(xz3, dt3, conv_w, conv_b, dt_bias, a_log, d_skip, norm_w, consts):
    y_part, act = _ssd_fwd(xz3, dt3, conv_w, conv_b, dt_bias, a_log, d_skip, consts)
    return _ssd_bwd(act, dt3, y_part, xz3, dt_bias, a_log, norm_w, consts)


def _ssd_consts():
    idx = np.arange(CHUNK)
    tril = (idx[None, :] <= idx[:, None]).astype(np.float32)
    head = np.arange(SSM_WIDTH) // SSM_HEAD_DIM
    rows = np.arange(LANES)
    ef = (rows[:, None] == head[None, :]).astype(np.float32)
    eb = (rows[:, None] == head[None, :] + N_SSM_HEADS).astype(np.float32)
    return tuple(jnp.asarray(m, bf16) for m in (tril, tril.T, ef, eb))


def _out_proj_kernel(a_ref, s_ref, w_ref, x_ref, g_ref, o_ref):
    mix = jnp.dot(a_ref[...], w_ref[:ATTN_WIDTH, :], preferred_element_type=f32)
    mix = mix + jnp.dot(s_ref[...], w_ref[ATTN_WIDTH:, :], preferred_element_type=f32)
    o_ref[...] = x_ref[...] + _rms(mix, g_ref[...])


def _out_proj(a2, s2, w_out, x2, gain, tm):
    rows = x2.shape[0]
    return pl.pallas_call(
        _out_proj_kernel,
        grid=(rows // tm,),
        in_specs=[
            pl.BlockSpec((tm, ATTN_WIDTH), lambda i: (i, 0)),
            pl.BlockSpec((tm, SSM_WIDTH), lambda i: (i, 0)),
            pl.BlockSpec((D_MODEL, D_MODEL), lambda i: (0, 0)),
            pl.BlockSpec((tm, D_MODEL), lambda i: (i, 0)),
            pl.BlockSpec((1, D_MODEL), lambda i: (0, 0)),
        ],
        out_specs=pl.BlockSpec((tm, D_MODEL), lambda i: (i, 0)),
        out_shape=jax.ShapeDtypeStruct((rows, D_MODEL), f32),
        compiler_params=_cparams(1),
        name="out_proj",
    )(a2, s2, w_out, x2, gain)


def _mlp_kernel(x_ref, gpre_ref, wup_ref, wdn_ref, gpost_ref, o_ref, h_ref):
    j = pl.program_id(1)

    @pl.when(j == 0)
    def _():
        h_ref[...] = _rms(x_ref[...], gpre_ref[...]).astype(bf16)
        o_ref[...] = jnp.zeros_like(o_ref)

    u = jnp.dot(h_ref[...], wup_ref[...], preferred_element_type=f32)
    u = jnp.square(jnp.maximum(u, 0.0)).astype(bf16)
    o_ref[...] += jnp.dot(u, wdn_ref[...], preferred_element_type=f32)

    @pl.when(j == pl.num_programs(1) - 1)
    def _():
        o_ref[...] = x_ref[...] + _rms(o_ref[...], gpost_ref[...])


def _mlp(x2, g_pre, w_up, w_down, g_post, tm, tf):
    rows = x2.shape[0]
    return pl.pallas_call(
        _mlp_kernel,
        grid=(rows // tm, D_FF // tf),
        in_specs=[
            pl.BlockSpec((tm, D_MODEL), lambda i, j: (i, 0)),
            pl.BlockSpec((1, D_MODEL), lambda i, j: (0, 0)),
            pl.BlockSpec((D_MODEL, tf), lambda i, j: (0, j)),
            pl.BlockSpec((tf, D_MODEL), lambda i, j: (j, 0)),
            pl.BlockSpec((1, D_MODEL), lambda i, j: (0, 0)),
        ],
        out_specs=pl.BlockSpec((tm, D_MODEL), lambda i, j: (i, 0)),
        out_shape=jax.ShapeDtypeStruct((rows, D_MODEL), f32),
        scratch_shapes=[pltpu.VMEM((tm, D_MODEL), bf16)],
        compiler_params=_cparams(2),
        name="mlp",
    )(x2, g_pre, w_up, w_down, g_post)


def _tile(n, pref):
    return pref if n % pref == 0 else n


def _trunk(x, layers, rel_bias, bias, consts, t_attn):
    bsz, seq = x.shape[0], x.shape[1]
    rows = bsz * seq
    tm = _tile(seq, 512)
    x2 = x.reshape(rows, D_MODEL)
    for lp in layers:
        qk, vt, xz, dt = _in_proj(x2, lp["pre_norm_mix"], lp["w_main"], lp["w_dt"], bsz, seq, tm)
        if tm != t_attn:
            vt = jnp.swapaxes(vt, 3, 4).reshape(bsz, N_ATTN_HEADS, seq // t_attn, t_attn, ATTN_V_DIM)
            vt = jnp.swapaxes(vt, 3, 4)
        qk4 = qk.reshape(bsz, seq // t_attn, t_attn, 2 * ATTN_WIDTH)
        a_out = _attention(qk4, vt, bias, rel_bias, lp["lam"], lp["attn_norm"], lp["lambda_init"], t_attn)
        s_out = _ssd(xz.reshape(bsz, seq, CONV_CH + SSM_WIDTH), dt.reshape(bsz, seq, LANES),
                     lp["conv_w"], lp["conv_b"], lp["dt_bias"], lp["a_log"], lp["d_skip"], lp["ssm_norm"],
                     consts)
        x2 = _out_proj(a_out.reshape(rows, ATTN_WIDTH), s_out.reshape(rows, SSM_WIDTH), lp["w_out"], x2,
                       lp["post_norm_mix"], tm)
        x2 = _mlp(x2, lp["pre_norm_mlp"], lp["w_up"], lp["w_down"], lp["post_norm_mlp"], tm, 512)
    return x2.reshape(bsz, seq, D_MODEL)


def _pad_lanes(v, width=LANES):
    return jnp.pad(v, (0, width - v.shape[0]))[None, :].astype(f32)


def kernel(x_prompt, x_sample, rel_bias, pre_norm_mix, w_in, lambda_q1, lambda_k1, lambda_q2, lambda_k2,
           attn_norm, conv_w, conv_b, dt_bias_fwd, dt_bias_bwd, a_log_fwd, a_log_bwd, d_skip, ssm_norm,
           w_out, post_norm_mix, pre_norm_mlp, w_up, w_down, post_norm_mlp):
    depth = w_in.shape[0]
    row = lambda v: v[None, :].astype(f32)
    layers = []
    for i in range(depth):
        lambda_init = 0.8 - 0.6 * math.exp(-0.3 * i)
        lam = (jnp.exp(jnp.sum(lambda_q1[i].astype(f32) * lambda_k1[i].astype(f32)))
               - jnp.exp(jnp.sum(lambda_q2[i].astype(f32) * lambda_k2[i].astype(f32))) + lambda_init)
        wi = w_in[i]
        w_main = jnp.concatenate([wi[:, OFF_Q:OFF_Z], wi[:, OFF_XBC:OFF_DT], wi[:, OFF_Z:OFF_XBC]],
                                 axis=1).astype(bf16)
        w_dt = jnp.pad(wi[:, OFF_DT:], ((0, 0), (0, LANES - 2 * N_SSM_HEADS))).astype(bf16)
        layers.append(dict(
            lambda_init=lambda_init,
            lam=lam.reshape(1).astype(f32),
            pre_norm_mix=row(pre_norm_mix[i]), w_main=w_main, w_dt=w_dt,
            attn_norm=row(attn_norm[i]),
            conv_w=jnp.pad(conv_w[i].astype(f32), ((0, 8 - D_CONV), (0, 0))), conv_b=row(conv_b[i]),
            dt_bias=_pad_lanes(jnp.concatenate([dt_bias_fwd[i], dt_bias_bwd[i]])),
            a_log=_pad_lanes(jnp.concatenate([a_log_fwd[i], a_log_bwd[i]])),
            d_skip=row(jnp.repeat(d_skip[i], SSM_HEAD_DIM)), ssm_norm=row(ssm_norm[i]),
            w_out=w_out[i].astype(bf16), post_norm_mix=row(post_norm_mix[i]),
            pre_norm_mlp=row(pre_norm_mlp[i]), w_up=w_up[i].astype(bf16), w_down=w_down[i].astype(bf16),
            post_norm_mlp=row(post_norm_mlp[i]),
        ))
    consts = _ssd_consts()
    rel = rel_bias.astype(f32)
    outs = []
    biases = {}
    for x in (x_prompt, x_sample):
        t_attn = _tile(x.shape[1], 512)
        if t_attn not in biases:
            biases[t_attn] = _bias_tiles(rel, t_attn)
        outs.append(_trunk(x, layers, rel, biases[t_attn], consts, t_attn))
    return tuple(outs)
```

```python
import functools
import math

import jax
import jax.numpy as jnp
import numpy as np
from jax import lax
from jax.experimental import pallas as pl
from jax.experimental.pallas import tpu as pltpu

f32 = jnp.float32
bf16 = jnp.bfloat16

D_MODEL = 2048
N_ATTN_HEADS = 8
ATTN_HEAD_DIM = 64
ATTN_V_DIM = 2 * ATTN_HEAD_DIM
ATTN_WIDTH = N_ATTN_HEADS * ATTN_V_DIM
SSM_WIDTH = 1024
SSM_HEAD_DIM = 64
N_SSM_HEADS = 16
N_SSM_GROUPS = 4
SSM_HPG = 4
D_STATE = 128
D_CONV = 5
CHUNK = 128
CONV_CH = SSM_WIDTH + 2 * N_SSM_GROUPS * D_STATE
GROUP_W = SSM_HPG * SSM_HEAD_DIM
OFF_Q, OFF_K, OFF_V, OFF_Z = 0, 1024, 2048, 3072
OFF_XBC = 4096
OFF_DT = OFF_XBC + CONV_CH
D_FF = 4 * D_MODEL
NUM_BUCKETS = 32
MAX_DISTANCE = 128
EPS = 1e-6
NEG = -1e30

LANES = 128
HALO = 16
VMEM_LIMIT = 56 * 1024 * 1024


def _cparams(n_axes):
    return pltpu.CompilerParams(dimension_semantics=("arbitrary",) * n_axes,
                                vmem_limit_bytes=VMEM_LIMIT)


def _rms(xf, g):
    ms = jnp.mean(xf * xf, axis=-1, keepdims=True)
    return (xf * lax.rsqrt(ms + EPS)) * g


def _split3(v):
    hi = v.astype(bf16)
    r1 = v - hi.astype(f32)
    mid = r1.astype(bf16)
    lo = (r1 - mid.astype(f32)).astype(bf16)
    return hi, mid, lo


def _dot3(v, m01):
    hi, mid, lo = _split3(v)
    d = lambda a: jnp.dot(a, m01, preferred_element_type=f32)
    return d(hi) + d(mid) + d(lo)


def _dot3_left(m01, v):
    hi, mid, lo = _split3(v)
    d = lambda a: jnp.dot(m01, a, preferred_element_type=f32)
    return d(hi) + d(mid) + d(lo)


def _in_proj_kernel(x_ref, g_ref, w_ref, wdt_ref, qk_ref, vt_ref, xz_ref, dt_ref, h_ref):
    j = pl.program_id(1)

    @pl.when(j == 0)
    def _():
        hb = _rms(x_ref[...], g_ref[...]).astype(bf16)
        h_ref[...] = hb
        dt_ref[...] = jnp.dot(hb, wdt_ref[...], preferred_element_type=f32)

    acc = jnp.dot(h_ref[...], w_ref[...], preferred_element_type=f32)

    @pl.when(j == 0)
    def _():
        qk_ref[...] = (acc * (ATTN_HEAD_DIM ** -0.5)).astype(bf16)

    @pl.when(j == 1)
    def _():
        qk_ref[...] = acc.astype(bf16)

    @pl.when(j == 2)
    def _():
        tm = acc.shape[0]
        vt_ref[0, :, 0] = acc.T.astype(bf16).reshape(N_ATTN_HEADS, ATTN_V_DIM, tm)

    @pl.when(j >= 3)
    def _():
        xz_ref[...] = acc.astype(bf16)


def _in_proj(x2, gain, w_main, w_dt, bsz, seq, tm):
    rows = x2.shape[0]
    nl = seq // tm
    nj = w_main.shape[1] // ATTN_WIDTH
    return pl.pallas_call(
        _in_proj_kernel,
        grid=(rows // tm, nj),
        in_specs=[
            pl.BlockSpec((tm, D_MODEL), lambda i, j: (i, 0)),
            pl.BlockSpec((1, D_MODEL), lambda i, j: (0, 0)),
            pl.BlockSpec((D_MODEL, ATTN_WIDTH), lambda i, j: (0, j)),
            pl.BlockSpec((D_MODEL, LANES), lambda i, j: (0, 0)),
        ],
        out_specs=[
            pl.BlockSpec((tm, ATTN_WIDTH), lambda i, j: (i, jnp.minimum(j, 1))),
            pl.BlockSpec((1, N_ATTN_HEADS, 1, ATTN_V_DIM, tm), lambda i, j: (i // nl, 0, i % nl, 0, 0)),
            pl.BlockSpec((tm, ATTN_WIDTH), lambda i, j: (i, jnp.clip(j - 3, 0, 2))),
            pl.BlockSpec((tm, LANES), lambda i, j: (i, 0)),
        ],
        out_shape=[
            jax.ShapeDtypeStruct((rows, 2 * ATTN_WIDTH), bf16),
            jax.ShapeDtypeStruct((bsz, N_ATTN_HEADS, nl, ATTN_V_DIM, tm), bf16),
            jax.ShapeDtypeStruct((rows, CONV_CH + SSM_WIDTH), bf16),
            jax.ShapeDtypeStruct((rows, LANES), f32),
        ],
        scratch_shapes=[pltpu.VMEM((tm, D_MODEL), bf16)],
        compiler_params=_cparams(2),
        name="in_proj",
    )(x2, gain, w_main, w_dt)


def _bias_kernel(tab_ref, o_ref, *, tq, tk):
    h = pl.program_id(0)
    d = pl.program_id(1) - 1
    kk = lax.broadcasted_iota(jnp.int32, (tk, tq), 0)
    qq = lax.broadcasted_iota(jnp.int32, (tk, tq), 1)
    rel = d * tk + kk - qq
    nb = NUM_BUCKETS // 2
    max_exact = nb // 2
    ret = jnp.where(rel > 0, nb, 0)
    n = jnp.abs(rel)
    nf = jnp.maximum(n, 1).astype(f32)
    large = max_exact + (jnp.log(nf / max_exact) / math.log(MAX_DISTANCE / max_exact)
                         * (nb - max_exact)).astype(jnp.int32)
    large = jnp.minimum(large, nb - 1)
    bucket = ret + jnp.where(n < max_exact, n, large)
    out = jnp.zeros((tk, tq), f32)
    for b in range(NUM_BUCKETS):
        out = jnp.where(bucket == b, tab_ref[b, h], out)
    o_ref[0, 0] = out


def _bias_tiles(rel_bias, tq, tk):
    n_near = tq // tk + 2
    return pl.pallas_call(
        functools.partial(_bias_kernel, tq=tq, tk=tk),
        grid=(N_ATTN_HEADS, n_near),
        in_specs=[pl.BlockSpec(memory_space=pltpu.SMEM)],
        out_specs=pl.BlockSpec((1, 1, tk, tq), lambda h, d: (h, d, 0, 0)),
        out_shape=jax.ShapeDtypeStruct((N_ATTN_HEADS, n_near, tk, tq), f32),
        compiler_params=_cparams(2),
        name="bias_tiles",
    )(rel_bias)


def _attn_kernel(tab_ref, lam_ref, q_ref, k_ref, vt_ref, bias_ref, w_ref, o_ref,
                 qcat_ref, s_ref, p_ref, m_ref, acc_ref, *, tq, tk, nk, out_scale):
    h = pl.program_id(1)
    qi = pl.program_id(2)
    r = tq // tk
    c_left = tab_ref[NUM_BUCKETS // 2 - 1, h]
    c_right = tab_ref[NUM_BUCKETS - 1, h]
    lam = lam_ref[0]

    qt = q_ref[0].astype(f32).T
    row = lax.broadcasted_iota(jnp.int32, qt.shape, 0)
    qcat_ref[...] = jnp.concatenate([jnp.where(row < ATTN_HEAD_DIM, qt, 0.0),
                                     jnp.where(row >= ATTN_HEAD_DIM, qt, 0.0)], axis=1).astype(bf16)
    ones = jnp.ones((8, tk), bf16)
    near_lo = jnp.maximum(qi * r - 1, 0)
    near_hi = jnp.minimum(qi * r + r + 1, nk)
    n_far = nk - (near_hi - near_lo)

    def logits(kt):
        return jnp.dot(k_ref[0, kt], qcat_ref[...], preferred_element_type=f32)

    def pv(kt, p):
        vext = jnp.concatenate([vt_ref[0, 0, kt], ones], axis=0)
        return jnp.dot(vext, p, preferred_element_type=f32)

    def far_kt(f):
        return f + jnp.where(f >= near_lo, near_hi - near_lo, 0)

    def far_c(f):
        return jnp.where(f >= near_lo, c_right, c_left)

    def stage_a(f):
        s = logits(far_kt(f))
        s_ref[...] = s
        return jnp.max(s, axis=0, keepdims=True)

    def stage_b(f, smax):
        c = far_c(f)
        m = m_ref[...]
        m_new = jnp.maximum(m, smax + c)
        p_ref[...] = jnp.exp(s_ref[...] - (m_new - c)).astype(bf16)
        m_ref[...] = m_new
        return jnp.exp(m - m_new)

    def stage_c(f, alpha):
        acc_ref[...] = acc_ref[...] * alpha + pv(far_kt(f), p_ref[...])

    m_ref[...] = jnp.full(m_ref.shape, NEG, f32)
    acc_ref[...] = jnp.zeros(acc_ref.shape, f32)

    smax = stage_a(0)
    alpha = stage_b(0, smax)
    smax = stage_a(1)

    def far_body(i, carry):
        alpha, smax = carry
        stage_c(i - 2, alpha)
        alpha = stage_b(i - 1, smax)
        smax = stage_a(i)
        return alpha, smax

    alpha, smax = lax.fori_loop(2, n_far, far_body, (alpha, smax))
    stage_c(n_far - 2, alpha)
    alpha = stage_b(n_far - 1, smax)
    stage_c(n_far - 1, alpha)

    def near(kt, carry):
        b = bias_ref[0, kt - qi * r + 1]
        s = logits(kt) + jnp.concatenate([b, b], axis=1)
        m = m_ref[...]
        m_new = jnp.maximum(m, jnp.max(s, axis=0, keepdims=True))
        p = jnp.exp(s - m_new).astype(bf16)
        acc_ref[...] = acc_ref[...] * jnp.exp(m - m_new) + pv(kt, p)
        m_ref[...] = m_new
        return carry

    lax.fori_loop(near_lo, near_hi, near, 0)

    acc = acc_ref[...]
    o = acc[:ATTN_V_DIM] * (1.0 / acc[ATTN_V_DIM:ATTN_V_DIM + 1])
    diff = (o[:, :tq] - lam * o[:, tq:]).T
    o_ref[0] = (_rms(diff, w_ref[...]) * out_scale).astype(bf16)


def _attention(qk3, vt, bias, rel_bias, lam, subln_w, lambda_init, tq, tk):
    bsz, seq = qk3.shape[0], qk3.shape[1]
    nk = seq // tk
    assert nk >= tq // tk + 4, "the far-tile pipeline needs at least two far key tiles per query tile"
    qk4 = qk3.reshape(bsz, nk, tk, 2 * ATTN_WIDTH)
    kern = functools.partial(_attn_kernel, tq=tq, tk=tk, nk=nk, out_scale=1.0 - lambda_init)
    return pl.pallas_call(
        kern,
        grid=(bsz, N_ATTN_HEADS, seq // tq),
        in_specs=[
            pl.BlockSpec(memory_space=pltpu.SMEM),
            pl.BlockSpec(memory_space=pltpu.SMEM),
            pl.BlockSpec((1, tq, ATTN_V_DIM), lambda b, h, i: (b, i, h)),
            pl.BlockSpec((1, nk, tk, ATTN_V_DIM), lambda b, h, i: (b, 0, 0, N_ATTN_HEADS + h)),
            pl.BlockSpec((1, 1, nk, ATTN_V_DIM, tk), lambda b, h, i: (b, h, 0, 0, 0)),
            pl.BlockSpec((1, tq // tk + 2, tk, tq), lambda b, h, i: (h, 0, 0, 0)),
            pl.BlockSpec((1, ATTN_V_DIM), lambda b, h, i: (0, 0)),
        ],
        out_specs=pl.BlockSpec((1, tq, ATTN_V_DIM), lambda b, h, i: (b, i, h)),
        out_shape=jax.ShapeDtypeStruct((bsz, seq, ATTN_WIDTH), bf16),
        scratch_shapes=[
            pltpu.VMEM((ATTN_V_DIM, 2 * tq), bf16),
            pltpu.VMEM((tk, 2 * tq), f32),
            pltpu.VMEM((tk, 2 * tq), bf16),
            pltpu.VMEM((1, 2 * tq), f32),
            pltpu.VMEM((ATTN_V_DIM + 8, 2 * tq), f32),
        ],
        compiler_params=_cparams(3),
        name="attention",
    )(rel_bias, lam, qk3, qk4, vt, bias, subln_w)


def _softplus(v):
    return jnp.maximum(v, 0.0) + jnp.log1p(jnp.exp(-jnp.abs(v)))


def _ssd_decay_terms(dt_ref, dtb_ref, alog_ref, tril_ref):
    dtv = _softplus(dt_ref[0] + dtb_ref[...])
    lane = lax.broadcasted_iota(jnp.int32, (1, LANES), 1)
    a_row = jnp.where(lane < 2 * N_SSM_HEADS, -jnp.exp(alog_ref[...]), 0.0)
    a = dtv * a_row
    cs = _dot3_left(tril_ref[...], a)
    return dtv, a, cs


def _ssd_fwd_kernel(cur_ref, prev_ref, next_ref, dt_ref, cw_ref, cb_ref, dtb_ref, alog_ref, dskip_ref,
                    tril_ref, triu_ref, ef_ref, y_ref, act_ref, state_ref):
    c = pl.program_id(1)
    nc = pl.num_programs(1)

    @pl.when(c == 0)
    def _():
        state_ref[...] = jnp.zeros_like(state_ref)

    pmask = (c > 0).astype(f32)
    nmask = (c < nc - 1).astype(f32)
    ext = jnp.concatenate([prev_ref[0, HALO - 8:, :].astype(f32) * pmask,
                           cur_ref[0].astype(f32),
                           next_ref[0, :8, :].astype(f32) * nmask], axis=0)
    conv = jnp.zeros((CHUNK, CONV_CH), f32) + cb_ref[...]
    for j in range(D_CONV):
        off = 8 + j - D_CONV // 2
        conv = conv + ext[off:off + CHUNK, :] * cw_ref[j:j + 1, :]
    act = conv * (1.0 / (1.0 + jnp.exp(-conv)))
    actb = act.astype(bf16)
    act_ref[0] = actb

    dtv, a, cs = _ssd_decay_terms(dt_ref, dtb_ref, alog_ref, tril_ref)
    a_t = a.T
    dt_t = dtv.T
    cs_t = _dot3(a_t, triu_ref[...])
    ecs = cs - a
    ecs_t = cs_t - a_t

    tot = cs[CHUNK - 1:CHUNK, :]
    scale_off = _dot3(jnp.exp(cs), ef_ref[...])
    scale_in = _dot3(dtv * jnp.exp(tot - cs), ef_ref[...])
    carry_dec = scale_off[CHUNK - 1:CHUNK, :]

    li = lax.broadcasted_iota(jnp.int32, (CHUNK, CHUNK), 0)
    si = lax.broadcasted_iota(jnp.int32, (CHUNK, CHUNK), 1)
    lower = li >= si
    upper = si >= li

    for g in range(N_SSM_GROUPS):
        gs = slice(g * GROUP_W, (g + 1) * GROUP_W)
        bg = actb[:, SSM_WIDTH + g * D_STATE: SSM_WIDTH + (g + 1) * D_STATE]
        cg = actb[:, SSM_WIDTH + N_SSM_GROUPS * D_STATE + g * D_STATE:
                  SSM_WIDTH + N_SSM_GROUPS * D_STATE + (g + 1) * D_STATE]
        cbm = lax.dot_general(cg, bg, (((1,), (1,)), ((), ())), preferred_element_type=f32)
        ws, xs_blocks = [], []
        for hh in range(SSM_HPG):
            hd = g * SSM_HPG + hh
            hb = N_SSM_HEADS + hd
            dec_f = jnp.exp(jnp.where(lower, cs[:, hd:hd + 1] - cs_t[hd:hd + 1, :], NEG))
            dec_b = jnp.exp(jnp.where(upper, ecs_t[hb:hb + 1, :] - ecs[:, hb:hb + 1], NEG))
            w = cbm * (dec_f * dt_t[hd:hd + 1, :] + dec_b * dt_t[hb:hb + 1, :])
            ws.append(w.astype(bf16))
            lane = lax.broadcasted_iota(jnp.int32, (CHUNK, GROUP_W), 1)
            own = (lane >= hh * SSM_HEAD_DIM) & (lane < (hh + 1) * SSM_HEAD_DIM)
            xs_blocks.append(jnp.where(own, actb[:, gs], jnp.zeros((), bf16)))
        xg = act[:, gs]
        wcat = jnp.concatenate(ws, axis=1)
        xblk = jnp.concatenate(xs_blocks, axis=0)
        y = jnp.dot(wcat, xblk, preferred_element_type=f32)
        y = y + jnp.dot(cg, state_ref[g].astype(bf16), preferred_element_type=f32) * scale_off[:, gs]
        y_ref[0, :, gs] = y + xg * dskip_ref[:, gs]
        xs = (xg * scale_in[:, gs]).astype(bf16)
        new = jnp.dot(bg.astype(f32).T.astype(bf16), xs, preferred_element_type=f32)
        state_ref[g] = state_ref[g] * carry_dec[:, gs] + new


def _ssd_bwd_kernel(act_ref, dt_ref, y_ref, z_ref, dtb_ref, alog_ref, nw_ref, tril_ref, eb_ref,
                    o_ref, state_ref):
    c = pl.program_id(1)

    @pl.when(c == 0)
    def _():
        state_ref[...] = jnp.zeros_like(state_ref)

    actb = act_ref[0]
    dtv, a, cs = _ssd_decay_terms(dt_ref, dtb_ref, alog_ref, tril_ref)
    ecs = cs - a
    tot = cs[CHUNK - 1:CHUNK, :]
    scale_off = _dot3(jnp.exp(tot - ecs), eb_ref[...])
    scale_in = _dot3(dtv * jnp.exp(ecs), eb_ref[...])
    carry_dec = scale_off[0:1, :]

    zf = z_ref[0].astype(f32)
    gate = zf * (1.0 / (1.0 + jnp.exp(-zf)))
    for g in range(N_SSM_GROUPS):
        gs = slice(g * GROUP_W, (g + 1) * GROUP_W)
        bg = actb[:, SSM_WIDTH + g * D_STATE: SSM_WIDTH + (g + 1) * D_STATE]
        cg = actb[:, SSM_WIDTH + N_SSM_GROUPS * D_STATE + g * D_STATE:
                  SSM_WIDTH + N_SSM_GROUPS * D_STATE + (g + 1) * D_STATE]
        y = y_ref[0, :, gs] + (jnp.dot(cg, state_ref[g].astype(bf16), preferred_element_type=f32)
                               * scale_off[:, gs])
        xs = (actb[:, gs].astype(f32) * scale_in[:, gs]).astype(bf16)
        new = jnp.dot(bg.astype(f32).T.astype(bf16), xs, preferred_element_type=f32)
        state_ref[g] = state_ref[g] * carry_dec[:, gs] + new
        o_ref[0, :, gs] = _rms(y * gate[:, gs], nw_ref[:, gs]).astype(bf16)


def _const_spec(shape):
    return pl.BlockSpec(shape, lambda b, c: (0,) * len(shape))


def _ssd_fwd(xz3, dt3, conv_w, conv_b, dt_bias, a_log, d_skip, consts):
    bsz, seq = xz3.shape[0], xz3.shape[1]
    nc = seq // CHUNK
    hb = CHUNK // HALO
    nh = seq // HALO
    tril, triu, ef, _ = consts
    const = _const_spec
    return pl.pallas_call(
        _ssd_fwd_kernel,
        grid=(bsz, nc),
        in_specs=[
            pl.BlockSpec((1, CHUNK, CONV_CH), lambda b, c: (b, c, 0)),
            pl.BlockSpec((1, HALO, CONV_CH), lambda b, c: (b, jnp.maximum(c * hb - 1, 0), 0)),
            pl.BlockSpec((1, HALO, CONV_CH), lambda b, c: (b, jnp.minimum((c + 1) * hb, nh - 1), 0)),
            pl.BlockSpec((1, CHUNK, LANES), lambda b, c: (b, c, 0)),
            const((8, CONV_CH)), const((1, CONV_CH)), const((1, LANES)), const((1, LANES)),
            const((1, SSM_WIDTH)), const((CHUNK, CHUNK)), const((CHUNK, CHUNK)), const((LANES, SSM_WIDTH)),
        ],
        out_specs=[
            pl.BlockSpec((1, CHUNK, SSM_WIDTH), lambda b, c: (b, c, 0)),
            pl.BlockSpec((1, CHUNK, CONV_CH), lambda b, c: (b, c, 0)),
        ],
        out_shape=[
            jax.ShapeDtypeStruct((bsz, seq, SSM_WIDTH), f32),
            jax.ShapeDtypeStruct((bsz, seq, CONV_CH), bf16),
        ],
        scratch_shapes=[pltpu.VMEM((N_SSM_GROUPS, D_STATE, GROUP_W), f32)],
        compiler_params=_cparams(2),
        name="ssd_fwd",
    )(xz3, xz3, xz3, dt3, conv_w, conv_b, dt_bias, a_log, d_skip, tril, triu, ef)


def _ssd_bwd(act, dt3, y_part, xz3, dt_bias, a_log, norm_w, consts):
    bsz, seq = xz3.shape[0], xz3.shape[1]
    nc = seq // CHUNK
    tril, _, _, eb = consts
    const = _const_spec
    rev = lambda b, c: (b, nc - 1 - c, 0)
    return pl.pallas_call(
        _ssd_bwd_kernel,
        grid=(bsz, nc),
        in_specs=[
            pl.BlockSpec((1, CHUNK, CONV_CH), rev),
            pl.BlockSpec((1, CHUNK, LANES), rev),
            pl.BlockSpec((1, CHUNK, SSM_WIDTH), rev),
            pl.BlockSpec((1, CHUNK, SSM_WIDTH), lambda b, c: (b, nc - 1 - c, CONV_CH // SSM_WIDTH)),
            const((1, LANES)), const((1, LANES)), const((1, SSM_WIDTH)),
            const((CHUNK, CHUNK)), const((LANES, SSM_WIDTH)),
        ],
        out_specs=pl.BlockSpec((1, CHUNK, SSM_WIDTH), rev),
        out_shape=jax.ShapeDtypeStruct((bsz, seq, SSM_WIDTH), bf16),
        scratch_shapes=[pltpu.VMEM((N_SSM_GROUPS, D_STATE, GROUP_W), f32)],
        compiler_params=_cparams(2),
        name="ssd_bwd",
    )(act, dt3, y_part, xz3, dt_bias, a_log, norm_w, tril, eb)


def _ssd(xz3, dt3, conv_w, conv_b, dt_bias, a_log, d_skip, norm_w, consts):
    y_part, act = _ssd_fwd(xz3, dt3, conv_w, conv_b, dt_bias, a_log, d_skip, consts)
    return _ssd_bwd(act, dt3, y_part, xz3, dt_bias, a_log, norm_w, consts)


def _ssd_consts():
    idx = np.arange(CHUNK)
    tril = (idx[None, :] <= idx[:, None]).astype(np.float32)
    head = np.arange(SSM_WIDTH) // SSM_HEAD_DIM
    rows = np.arange(LANES)
    ef = (rows[:, None] == head[None, :]).astype(np.float32)
    eb = (rows[:, None] == head[None, :] + N_SSM_HEADS).astype(np.float32)
    return tuple(jnp.asarray(m, bf16) for m in (tril, tril.T, ef, eb))


def _out_proj_kernel(a_ref, s_ref, w_ref, x_ref, g_ref, o_ref):
    mix = jnp.dot(a_ref[...], w_ref[:ATTN_WIDTH, :], preferred_element_type=f32)
    mix = mix + jnp.dot(s_ref[...], w_ref[ATTN_WIDTH:, :], preferred_element_type=f32)
    o_ref[...] = x_ref[...] + _rms(mix, g_ref[...])


def _out_proj(a2, s2, w_out, x2, gain, tm):
    rows = x2.shape[0]
    return pl.pallas_call(
        _out_proj_kernel,
        grid=(rows // tm,),
        in_specs=[
            pl.BlockSpec((tm, ATTN_WIDTH), lambda i: (i, 0)),
            pl.BlockSpec((tm, SSM_WIDTH), lambda i: (i, 0)),
            pl.BlockSpec((D_MODEL, D_MODEL), lambda i: (0, 0)),
            pl.BlockSpec((tm, D_MODEL), lambda i: (i, 0)),
            pl.BlockSpec((1, D_MODEL), lambda i: (0, 0)),
        ],
        out_specs=pl.BlockSpec((tm, D_MODEL), lambda i: (i, 0)),
        out_shape=jax.ShapeDtypeStruct((rows, D_MODEL), f32),
        compiler_params=_cparams(1),
        name="out_proj",
    )(a2, s2, w_out, x2, gain)


def _mlp_kernel(x_ref, gpre_ref, wup_ref, wdn_ref, gpost_ref, o_ref, h_ref):
    j = pl.program_id(1)

    @pl.when(j == 0)
    def _():
        h_ref[...] = _rms(x_ref[...], gpre_ref[...]).astype(bf16)
        o_ref[...] = jnp.zeros_like(o_ref)

    u = jnp.dot(h_ref[...], wup_ref[...], preferred_element_type=f32)
    u = jnp.square(jnp.maximum(u, 0.0)).astype(bf16)
    o_ref[...] += jnp.dot(u, wdn_ref[...], preferred_element_type=f32)

    @pl.when(j == pl.num_programs(1) - 1)
    def _():
        o_ref[...] = x_ref[...] + _rms(o_ref[...], gpost_ref[...])


def _mlp(x2, g_pre, w_up, w_down, g_post, tm, tf):
    rows = x2.shape[0]
    return pl.pallas_call(
        _mlp_kernel,
        grid=(rows // tm, D_FF // tf),
        in_specs=[
            pl.BlockSpec((tm, D_MODEL), lambda i, j: (i, 0)),
            pl.BlockSpec((1, D_MODEL), lambda i, j: (0, 0)),
            pl.BlockSpec((D_MODEL, tf), lambda i, j: (0, j)),
            pl.BlockSpec((tf, D_MODEL), lambda i, j: (j, 0)),
            pl.BlockSpec((1, D_MODEL), lambda i, j: (0, 0)),
        ],
        out_specs=pl.BlockSpec((tm, D_MODEL), lambda i, j: (i, 0)),
        out_shape=jax.ShapeDtypeStruct((rows, D_MODEL), f32),
        scratch_shapes=[pltpu.VMEM((tm, D_MODEL), bf16)],
        compiler_params=_cparams(2),
        name="mlp",
    )(x2, g_pre, w_up, w_down, g_post)


def _attn_tiles(seq):
    for tq, tk in ((1024, 512), (512, 512), (256, 256), (128, 128)):
        if seq % tq == 0 and seq // tk >= tq // tk + 4:
            return tq, tk
    raise ValueError(f"sequence length {seq} is too short for the attention tiling")


def _trunk(x, layers, rel_bias, bias, consts, tq, tk):
    bsz, seq = x.shape[0], x.shape[1]
    rows = bsz * seq
    tm = tk
    x2 = x.reshape(rows, D_MODEL)
    for lp in layers:
        qk, vt, xz, dt = _in_proj(x2, lp["pre_norm_mix"], lp["w_main"], lp["w_dt"], bsz, seq, tm)
        a_out = _attention(qk.reshape(bsz, seq, 2 * ATTN_WIDTH), vt, bias, rel_bias, lp["lam"], lp["attn_norm"],
                           lp["lambda_init"], tq, tk)
        s_out = _ssd(xz.reshape(bsz, seq, CONV_CH + SSM_WIDTH), dt.reshape(bsz, seq, LANES),
                     lp["conv_w"], lp["conv_b"], lp["dt_bias"], lp["a_log"], lp["d_skip"], lp["ssm_norm"],
                     consts)
        x2 = _out_proj(a_out.reshape(rows, ATTN_WIDTH), s_out.reshape(rows, SSM_WIDTH), lp["w_out"], x2,
                       lp["post_norm_mix"], tm)
        x2 = _mlp(x2, lp["pre_norm_mlp"], lp["w_up"], lp["w_down"], lp["post_norm_mlp"], tm, 512)
    return x2.reshape(bsz, seq, D_MODEL)


def _pad_lanes(v, width=LANES):
    return jnp.pad(v, (0, width - v.shape[0]))[None, :].astype(f32)


def kernel(x_prompt, x_sample, rel_bias, pre_norm_mix, w_in, lambda_q1, lambda_k1, lambda_q2, lambda_k2,
           attn_norm, conv_w, conv_b, dt_bias_fwd, dt_bias_bwd, a_log_fwd, a_log_bwd, d_skip, ssm_norm,
           w_out, post_norm_mix, pre_norm_mlp, w_up, w_down, post_norm_mlp):
    depth = w_in.shape[0]
    row = lambda v: v[None, :].astype(f32)
    layers = []
    for i in range(depth):
        lambda_init = 0.8 - 0.6 * math.exp(-0.3 * i)
        lam = (jnp.exp(jnp.sum(lambda_q1[i].astype(f32) * lambda_k1[i].astype(f32)))
               - jnp.exp(jnp.sum(lambda_q2[i].astype(f32) * lambda_k2[i].astype(f32))) + lambda_init)
        wi = w_in[i]
        w_main = jnp.concatenate([wi[:, OFF_Q:OFF_Z], wi[:, OFF_XBC:OFF_DT], wi[:, OFF_Z:OFF_XBC]],
                                 axis=1).astype(bf16)
        w_dt = jnp.pad(wi[:, OFF_DT:], ((0, 0), (0, LANES - 2 * N_SSM_HEADS))).astype(bf16)
        layers.append(dict(
            lambda_init=lambda_init,
            lam=lam.reshape(1).astype(f32),
            pre_norm_mix=row(pre_norm_mix[i]), w_main=w_main, w_dt=w_dt,
            attn_norm=row(attn_norm[i]),
            conv_w=jnp.pad(conv_w[i].astype(f32), ((0, 8 - D_CONV), (0, 0))), conv_b=row(conv_b[i]),
            dt_bias=_pad_lanes(jnp.concatenate([dt_bias_fwd[i], dt_bias_bwd[i]])),
            a_log=_pad_lanes(jnp.concatenate([a_log_fwd[i], a_log_bwd[i]])),
            d_skip=row(jnp.repeat(d_skip[i], SSM_HEAD_DIM)), ssm_norm=row(ssm_norm[i]),
            w_out=w_out[i].astype(bf16), post_norm_mix=row(post_norm_mix[i]),
            pre_norm_mlp=row(pre_norm_mlp[i]), w_up=w_up[i].astype(bf16), w_down=w_down[i].astype(bf16),
            post_norm_mlp=row(post_norm_mlp[i]),
        ))
    consts = _ssd_consts()
    rel = rel_bias.astype(f32)
    outs = []
    biases = {}
    for x in (x_prompt, x_sample):
        tq, tk = _attn_tiles(x.shape[1])
        if (tq, tk) not in biases:
            biases[(tq, tk)] = _bias_tiles(rel, tq, tk)
        outs.append(_trunk(x, layers, rel, biases[(tq, tk)], consts, tq, tk))
    return tuple(outs)
```

```python
import functools
import math

import jax
import jax.numpy as jnp
import numpy as np
from jax import lax
from jax.experimental import pallas as pl
from jax.experimental.pallas import tpu as pltpu

f32 = jnp.float32
bf16 = jnp.bfloat16

D_MODEL = 2048
N_ATTN_HEADS = 8
ATTN_HEAD_DIM = 64
ATTN_V_DIM = 2 * ATTN_HEAD_DIM
ATTN_WIDTH = N_ATTN_HEADS * ATTN_V_DIM
SSM_WIDTH = 1024
SSM_HEAD_DIM = 64
N_SSM_HEADS = 16
N_SSM_GROUPS = 4
SSM_HPG = 4
D_STATE = 128
D_CONV = 5
CHUNK = 128
CONV_CH = SSM_WIDTH + 2 * N_SSM_GROUPS * D_STATE
GROUP_W = SSM_HPG * SSM_HEAD_DIM
OFF_Q, OFF_K, OFF_V, OFF_Z = 0, 1024, 2048, 3072
OFF_XBC = 4096
OFF_DT = OFF_XBC + CONV_CH
D_FF = 4 * D_MODEL
NUM_BUCKETS = 32
MAX_DISTANCE = 128
EPS = 1e-6
NEG = -1e30

LANES = 128
MXU_COLS = 256
HALO = 16
VMEM_LIMIT = 56 * 1024 * 1024


def _cparams(n_axes):
    return pltpu.CompilerParams(dimension_semantics=("arbitrary",) * n_axes,
                                vmem_limit_bytes=VMEM_LIMIT)


def _rms(xf, g):
    ms = jnp.mean(xf * xf, axis=-1, keepdims=True)
    return (xf * lax.rsqrt(ms + EPS)) * g


def _split3(v):
    hi = v.astype(bf16)
    r1 = v - hi.astype(f32)
    mid = r1.astype(bf16)
    lo = (r1 - mid.astype(f32)).astype(bf16)
    return hi, mid, lo


def _dot3(v, m01):
    hi, mid, lo = _split3(v)
    d = lambda a: jnp.dot(a, m01, preferred_element_type=f32)
    return d(hi) + d(mid) + d(lo)


def _dot3_left(m01, v):
    hi, mid, lo = _split3(v)
    d = lambda a: jnp.dot(m01, a, preferred_element_type=f32)
    return d(hi) + d(mid) + d(lo)


def _in_proj_kernel(x_ref, g_ref, w_ref, wdt_ref, qk_ref, vt_ref, xz_ref, dt_ref, h_ref):
    j = pl.program_id(1)

    @pl.when(j == 0)
    def _():
        hb = _rms(x_ref[...], g_ref[...]).astype(bf16)
        h_ref[...] = hb
        dt_ref[...] = jnp.dot(hb, wdt_ref[...], preferred_element_type=f32)

    acc = jnp.dot(h_ref[...], w_ref[...], preferred_element_type=f32)

    @pl.when(j == 0)
    def _():
        qk_ref[...] = (acc * (ATTN_HEAD_DIM ** -0.5)).astype(bf16)

    @pl.when(j == 1)
    def _():
        qk_ref[...] = acc.astype(bf16)

    @pl.when(j == 2)
    def _():
        tm = acc.shape[0]
        vt_ref[0, :, 0] = acc.T.astype(bf16).reshape(N_ATTN_HEADS, ATTN_V_DIM, tm)

    @pl.when(j >= 3)
    def _():
        xz_ref[...] = acc.astype(bf16)


def _in_proj(x2, gain, w_main, w_dt, bsz, seq, tm):
    rows = x2.shape[0]
    nl = seq // tm
    nj = w_main.shape[1] // ATTN_WIDTH
    return pl.pallas_call(
        _in_proj_kernel,
        grid=(rows // tm, nj),
        in_specs=[
            pl.BlockSpec((tm, D_MODEL), lambda i, j: (i, 0)),
            pl.BlockSpec((1, D_MODEL), lambda i, j: (0, 0)),
            pl.BlockSpec((D_MODEL, ATTN_WIDTH), lambda i, j: (0, j)),
            pl.BlockSpec((D_MODEL, LANES), lambda i, j: (0, 0)),
        ],
        out_specs=[
            pl.BlockSpec((tm, ATTN_WIDTH), lambda i, j: (i, jnp.minimum(j, 1))),
            pl.BlockSpec((1, N_ATTN_HEADS, 1, ATTN_V_DIM, tm), lambda i, j: (i // nl, 0, i % nl, 0, 0)),
            pl.BlockSpec((tm, ATTN_WIDTH), lambda i, j: (i, jnp.clip(j - 3, 0, 2))),
            pl.BlockSpec((tm, LANES), lambda i, j: (i, 0)),
        ],
        out_shape=[
            jax.ShapeDtypeStruct((rows, 2 * ATTN_WIDTH), bf16),
            jax.ShapeDtypeStruct((bsz, N_ATTN_HEADS, nl, ATTN_V_DIM, tm), bf16),
            jax.ShapeDtypeStruct((rows, CONV_CH + SSM_WIDTH), bf16),
            jax.ShapeDtypeStruct((rows, LANES), f32),
        ],
        scratch_shapes=[pltpu.VMEM((tm, D_MODEL), bf16)],
        compiler_params=_cparams(2),
        name="in_proj",
    )(x2, gain, w_main, w_dt)


def _bias_kernel(tab_ref, o_ref, *, tq, tk):
    h = pl.program_id(0)
    d = pl.program_id(1) - 1
    kk = lax.broadcasted_iota(jnp.int32, (tk, tq), 0)
    qq = lax.broadcasted_iota(jnp.int32, (tk, tq), 1)
    rel = d * tk + kk - qq
    nb = NUM_BUCKETS // 2
    max_exact = nb // 2
    ret = jnp.where(rel > 0, nb, 0)
    n = jnp.abs(rel)
    nf = jnp.maximum(n, 1).astype(f32)
    large = max_exact + (jnp.log(nf / max_exact) / math.log(MAX_DISTANCE / max_exact)
                         * (nb - max_exact)).astype(jnp.int32)
    large = jnp.minimum(large, nb - 1)
    bucket = ret + jnp.where(n < max_exact, n, large)
    out = jnp.zeros((tk, tq), f32)
    for b in range(NUM_BUCKETS):
        out = jnp.where(bucket == b, tab_ref[b, h], out)
    o_ref[0, 0] = out


def _bias_tiles(rel_bias, tq, tk):
    n_near = tq // tk + 2
    return pl.pallas_call(
        functools.partial(_bias_kernel, tq=tq, tk=tk),
        grid=(N_ATTN_HEADS, n_near),
        in_specs=[pl.BlockSpec(memory_space=pltpu.SMEM)],
        out_specs=pl.BlockSpec((1, 1, tk, tq), lambda h, d: (h, d, 0, 0)),
        out_shape=jax.ShapeDtypeStruct((N_ATTN_HEADS, n_near, tk, tq), f32),
        compiler_params=_cparams(2),
        name="bias_tiles",
    )(rel_bias)


def _attn_kernel(tab_ref, lam_ref, q_ref, k_ref, vt_ref, bias_ref, w_ref, o_ref,
                 qcat_ref, s_ref, p_ref, smax_ref, alpha_ref, m_ref, acc_ref, *, tq, tk, nk, out_scale):
    h = pl.program_id(1)
    qi = pl.program_id(2)
    r = tq // tk
    c_left = tab_ref[NUM_BUCKETS // 2 - 1, h]
    c_right = tab_ref[NUM_BUCKETS - 1, h]
    lam = lam_ref[0]

    qt = q_ref[0].astype(f32).T
    row = lax.broadcasted_iota(jnp.int32, qt.shape, 0)
    qcat_ref[...] = jnp.concatenate([jnp.where(row < ATTN_HEAD_DIM, qt, 0.0),
                                     jnp.where(row >= ATTN_HEAD_DIM, qt, 0.0)], axis=1).astype(bf16)
    ones = jnp.ones((8, tk), bf16)
    near_lo = jnp.maximum(qi * r - 1, 0)
    near_hi = jnp.minimum(qi * r + r + 1, nk)
    n_far = nk - (near_hi - near_lo)

    def far_kt(f):
        return f + jnp.where(f >= near_lo, near_hi - near_lo, 0)

    def far_c(f):
        return jnp.where(f >= near_lo, c_right, c_left)

    n_blk = 2 * tq // MXU_COLS
    blk = lambda j: slice(j * MXU_COLS, (j + 1) * MXU_COLS)

    def stage_a(k, j):
        s = jnp.dot(k, qcat_ref[:, blk(j)], preferred_element_type=f32)
        s_ref[j] = s
        smax_ref[:, blk(j)] = jnp.max(s, axis=0, keepdims=True)

    def stage_b(c, j, after):
        m = m_ref[:, blk(j)]
        m_new = jnp.maximum(m, smax_ref[:, blk(j)] + c)
        p = jnp.exp(s_ref[j] - (m_new - c)).astype(bf16)
        if after is not None:
            p = p + (after * 0.0).astype(bf16)
        p_ref[j] = p
        m_ref[:, blk(j)] = m_new
        alpha_ref[:, blk(j)] = jnp.exp(m - m_new)

    def stage_c(vext, j):
        acc = acc_ref[:, blk(j)] * alpha_ref[:, blk(j)] + jnp.dot(vext, p_ref[j], preferred_element_type=f32)
        acc_ref[:, blk(j)] = acc
        return acc[ATTN_V_DIM:ATTN_V_DIM + 1]

    def far_steps(fa, fb, fc):
        k = None if fa is None else k_ref[0, far_kt(fa)]
        c = None if fb is None else far_c(fb)
        vext = None if fc is None else jnp.concatenate([vt_ref[0, 0, far_kt(fc)], ones], axis=0)
        for j in range(n_blk):
            row_sum = None if fc is None else stage_c(vext, j)
            if fb is not None:
                stage_b(c, j, row_sum)
            if fa is not None:
                stage_a(k, j)

    m_ref[...] = jnp.full(m_ref.shape, NEG, f32)
    acc_ref[...] = jnp.zeros(acc_ref.shape, f32)

    far_steps(0, None, None)
    far_steps(1, 0, None)

    def far_body(i, carry):
        far_steps(i, i - 1, i - 2)
        return carry

    lax.fori_loop(2, n_far, far_body, 0)
    far_steps(None, n_far - 1, n_far - 2)
    far_steps(None, None, n_far - 1)

    def near(kt, carry):
        k = k_ref[0, kt]
        vext = jnp.concatenate([vt_ref[0, 0, kt], ones], axis=0)
        bias = bias_ref[0, kt - qi * r + 1]
        s = jnp.dot(k, qcat_ref[...], preferred_element_type=f32) + jnp.concatenate([bias, bias], axis=1)
        m = m_ref[...]
        m_new = jnp.maximum(m, jnp.max(s, axis=0, keepdims=True))
        p = jnp.exp(s - m_new).astype(bf16)
        acc_ref[...] = acc_ref[...] * jnp.exp(m - m_new) + jnp.dot(vext, p, preferred_element_type=f32)
        m_ref[...] = m_new
        return carry

    lax.fori_loop(near_lo, near_hi, near, 0)

    acc = acc_ref[...]
    o = acc[:ATTN_V_DIM] * (1.0 / acc[ATTN_V_DIM:ATTN_V_DIM + 1])
    diff = (o[:, :tq] - lam * o[:, tq:]).T
    o_ref[0] = (_rms(diff, w_ref[...]) * out_scale).astype(bf16)


def _attention(qk3, vt, bias, rel_bias, lam, subln_w, lambda_init, tq, tk):
    bsz, seq = qk3.shape[0], qk3.shape[1]
    nk = seq // tk
    assert nk >= tq // tk + 4, "the far-tile pipeline needs at least two far key tiles per query tile"
    qk4 = qk3.reshape(bsz, nk, tk, 2 * ATTN_WIDTH)
    kern = functools.partial(_attn_kernel, tq=tq, tk=tk, nk=nk, out_scale=1.0 - lambda_init)
    once = pl.Buffered(1)
    return pl.pallas_call(
        kern,
        grid=(bsz, N_ATTN_HEADS, seq // tq),
        in_specs=[
            pl.BlockSpec(memory_space=pltpu.SMEM),
            pl.BlockSpec(memory_space=pltpu.SMEM),
            pl.BlockSpec((1, tq, ATTN_V_DIM), lambda b, h, i: (b, i, h)),
            pl.BlockSpec((1, nk, tk, ATTN_V_DIM), lambda b, h, i: (b, 0, 0, N_ATTN_HEADS + h), once),
            pl.BlockSpec((1, 1, nk, ATTN_V_DIM, tk), lambda b, h, i: (b, h, 0, 0, 0), once),
            pl.BlockSpec((1, tq // tk + 2, tk, tq), lambda b, h, i: (h, 0, 0, 0), once),
            pl.BlockSpec((1, ATTN_V_DIM), lambda b, h, i: (0, 0)),
        ],
        out_specs=pl.BlockSpec((1, tq, ATTN_V_DIM), lambda b, h, i: (b, i, h)),
        out_shape=jax.ShapeDtypeStruct((bsz, seq, ATTN_WIDTH), bf16),
        scratch_shapes=[
            pltpu.VMEM((ATTN_V_DIM, 2 * tq), bf16),
            pltpu.VMEM((2 * tq // MXU_COLS, tk, MXU_COLS), f32),
            pltpu.VMEM((2 * tq // MXU_COLS, tk, MXU_COLS), bf16),
            pltpu.VMEM((1, 2 * tq), f32),
            pltpu.VMEM((1, 2 * tq), f32),
            pltpu.VMEM((1, 2 * tq), f32),
            pltpu.VMEM((ATTN_V_DIM + 8, 2 * tq), f32),
        ],
        compiler_params=_cparams(3),
        name="attention",
    )(rel_bias, lam, qk3, qk4, vt, bias, subln_w)


def _softplus(v):
    return jnp.maximum(v, 0.0) + jnp.log1p(jnp.exp(-jnp.abs(v)))


def _ssd_decay_terms(dt_ref, dtb_ref, alog_ref, tril_ref):
    dtv = _softplus(dt_ref[0] + dtb_ref[...])
    lane = lax.broadcasted_iota(jnp.int32, (1, LANES), 1)
    a_row = jnp.where(lane < 2 * N_SSM_HEADS, -jnp.exp(alog_ref[...]), 0.0)
    a = dtv * a_row
    cs = _dot3_left(tril_ref[...], a)
    return dtv, a, cs


def _ssd_fwd_kernel(cur_ref, prev_ref, next_ref, dt_ref, cw_ref, cb_ref, dtb_ref, alog_ref, dskip_ref,
                    tril_ref, triu_ref, ef_ref, y_ref, act_ref, state_ref):
    c = pl.program_id(1)
    nc = pl.num_programs(1)

    @pl.when(c == 0)
    def _():
        state_ref[...] = jnp.zeros_like(state_ref)

    pmask = (c > 0).astype(f32)
    nmask = (c < nc - 1).astype(f32)
    ext = jnp.concatenate([prev_ref[0, HALO - 8:, :].astype(f32) * pmask,
                           cur_ref[0].astype(f32),
                           next_ref[0, :8, :].astype(f32) * nmask], axis=0)
    conv = jnp.zeros((CHUNK, CONV_CH), f32) + cb_ref[...]
    for j in range(D_CONV):
        off = 8 + j - D_CONV // 2
        conv = conv + ext[off:off + CHUNK, :] * cw_ref[j:j + 1, :]
    act = conv * (1.0 / (1.0 + jnp.exp(-conv)))
    actb = act.astype(bf16)
    act_ref[0] = actb

    dtv, a, cs = _ssd_decay_terms(dt_ref, dtb_ref, alog_ref, tril_ref)
    a_t = a.T
    dt_t = dtv.T
    cs_t = _dot3(a_t, triu_ref[...])
    ecs = cs - a
    ecs_t = cs_t - a_t

    tot = cs[CHUNK - 1:CHUNK, :]
    scale_off = _dot3(jnp.exp(cs), ef_ref[...])
    scale_in = _dot3(dtv * jnp.exp(tot - cs), ef_ref[...])
    carry_dec = scale_off[CHUNK - 1:CHUNK, :]

    li = lax.broadcasted_iota(jnp.int32, (CHUNK, CHUNK), 0)
    si = lax.broadcasted_iota(jnp.int32, (CHUNK, CHUNK), 1)
    lower = li >= si
    upper = si >= li

    for g in range(N_SSM_GROUPS):
        gs = slice(g * GROUP_W, (g + 1) * GROUP_W)
        bg = actb[:, SSM_WIDTH + g * D_STATE: SSM_WIDTH + (g + 1) * D_STATE]
        cg = actb[:, SSM_WIDTH + N_SSM_GROUPS * D_STATE + g * D_STATE:
                  SSM_WIDTH + N_SSM_GROUPS * D_STATE + (g + 1) * D_STATE]
        cbm = lax.dot_general(cg, bg, (((1,), (1,)), ((), ())), preferred_element_type=f32)
        ws, xs_blocks = [], []
        for hh in range(SSM_HPG):
            hd = g * SSM_HPG + hh
            hb = N_SSM_HEADS + hd
            dec_f = jnp.exp(jnp.where(lower, cs[:, hd:hd + 1] - cs_t[hd:hd + 1, :], NEG))
            dec_b = jnp.exp(jnp.where(upper, ecs_t[hb:hb + 1, :] - ecs[:, hb:hb + 1], NEG))
            w = cbm * (dec_f * dt_t[hd:hd + 1, :] + dec_b * dt_t[hb:hb + 1, :])
            ws.append(w.astype(bf16))
            lane = lax.broadcasted_iota(jnp.int32, (CHUNK, GROUP_W), 1)
            own = (lane >= hh * SSM_HEAD_DIM) & (lane < (hh + 1) * SSM_HEAD_DIM)
            xs_blocks.append(jnp.where(own, actb[:, gs], jnp.zeros((), bf16)))
        xg = act[:, gs]
        wcat = jnp.concatenate(ws, axis=1)
        xblk = jnp.concatenate(xs_blocks, axis=0)
        y = jnp.dot(wcat, xblk, preferred_element_type=f32)
        y = y + jnp.dot(cg, state_ref[g].astype(bf16), preferred_element_type=f32) * scale_off[:, gs]
        y_ref[0, :, gs] = y + xg * dskip_ref[:, gs]
        xs = (xg * scale_in[:, gs]).astype(bf16)
        new = jnp.dot(bg.astype(f32).T.astype(bf16), xs, preferred_element_type=f32)
        state_ref[g] = state_ref[g] * carry_dec[:, gs] + new


def _ssd_bwd_kernel(act_ref, dt_ref, y_ref, z_ref, dtb_ref, alog_ref, nw_ref, tril_ref, eb_ref,
                    o_ref, state_ref):
    c = pl.program_id(1)

    @pl.when(c == 0)
    def _():
        state_ref[...] = jnp.zeros_like(state_ref)

    actb = act_ref[0]
    dtv, a, cs = _ssd_decay_terms(dt_ref, dtb_ref, alog_ref, tril_ref)
    ecs = cs - a
    tot = cs[CHUNK - 1:CHUNK, :]
    scale_off = _dot3(jnp.exp(tot - ecs), eb_ref[...])
    scale_in = _dot3(dtv * jnp.exp(ecs), eb_ref[...])
    carry_dec = scale_off[0:1, :]

    zf = z_ref[0].astype(f32)
    gate = zf * (1.0 / (1.0 + jnp.exp(-zf)))
    for g in range(N_SSM_GROUPS):
        gs = slice(g * GROUP_W, (g + 1) * GROUP_W)
        bg = actb[:, SSM_WIDTH + g * D_STATE: SSM_WIDTH + (g + 1) * D_STATE]
        cg = actb[:, SSM_WIDTH + N_SSM_GROUPS * D_STATE + g * D_STATE:
                  SSM_WIDTH + N_SSM_GROUPS * D_STATE + (g + 1) * D_STATE]
        y = y_ref[0, :, gs] + (jnp.dot(cg, state_ref[g].astype(bf16), preferred_element_type=f32)
                               * scale_off[:, gs])
        xs = (actb[:, gs].astype(f32) * scale_in[:, gs]).astype(bf16)
        new = jnp.dot(bg.astype(f32).T.astype(bf16), xs, preferred_element_type=f32)
        state_ref[g] = state_ref[g] * carry_dec[:, gs] + new
        o_ref[0, :, gs] = _rms(y * gate[:, gs], nw_ref[:, gs]).astype(bf16)


def _const_spec(shape):
    return pl.BlockSpec(shape, lambda b, c: (0,) * len(shape))


def _ssd_fwd(xz3, dt3, conv_w, conv_b, dt_bias, a_log, d_skip, consts):
    bsz, seq = xz3.shape[0], xz3.shape[1]
    nc = seq // CHUNK
    hb = CHUNK // HALO
    nh = seq // HALO
    tril, triu, ef, _ = consts
    const = _const_spec
    return pl.pallas_call(
        _ssd_fwd_kernel,
        grid=(bsz, nc),
        in_specs=[
            pl.BlockSpec((1, CHUNK, CONV_CH), lambda b, c: (b, c, 0)),
            pl.BlockSpec((1, HALO, CONV_CH), lambda b, c: (b, jnp.maximum(c * hb - 1, 0), 0)),
            pl.BlockSpec((1, HALO, CONV_CH), lambda b, c: (b, jnp.minimum((c + 1) * hb, nh - 1), 0)),
            pl.BlockSpec((1, CHUNK, LANES), lambda b, c: (b, c, 0)),
            const((8, CONV_CH)), const((1, CONV_CH)), const((1, LANES)), const((1, LANES)),
            const((1, SSM_WIDTH)), const((CHUNK, CHUNK)), const((CHUNK, CHUNK)), const((LANES, SSM_WIDTH)),
        ],
        out_specs=[
            pl.BlockSpec((1, CHUNK, SSM_WIDTH), lambda b, c: (b, c, 0)),
            pl.BlockSpec((1, CHUNK, CONV_CH), lambda b, c: (b, c, 0)),
        ],
        out_shape=[
            jax.ShapeDtypeStruct((bsz, seq, SSM_WIDTH), f32),
            jax.ShapeDtypeStruct((bsz, seq, CONV_CH), bf16),
        ],
        scratch_shapes=[pltpu.VMEM((N_SSM_GROUPS, D_STATE, GROUP_W), f32)],
        compiler_params=_cparams(2),
        name="ssd_fwd",
    )(xz3, xz3, xz3, dt3, conv_w, conv_b, dt_bias, a_log, d_skip, tril, triu, ef)


def _ssd_bwd(act, dt3, y_part, xz3, dt_bias, a_log, norm_w, consts):
    bsz, seq = xz3.shape[0], xz3.shape[1]
    nc = seq // CHUNK
    tril, _, _, eb = consts
    const = _const_spec
    rev = lambda b, c: (b, nc - 1 - c, 0)
    return pl.pallas_call(
        _ssd_bwd_kernel,
        grid=(bsz, nc),
        in_specs=[
            pl.BlockSpec((1, CHUNK, CONV_CH), rev),
            pl.BlockSpec((1, CHUNK, LANES), rev),
            pl.BlockSpec((1, CHUNK, SSM_WIDTH), rev),
            pl.BlockSpec((1, CHUNK, SSM_WIDTH), lambda b, c: (b, nc - 1 - c, CONV_CH // SSM_WIDTH)),
            const((1, LANES)), const((1, LANES)), const((1, SSM_WIDTH)),
            const((CHUNK, CHUNK)), const((LANES, SSM_WIDTH)),
        ],
        out_specs=pl.BlockSpec((1, CHUNK, SSM_WIDTH), rev),
        out_shape=jax.ShapeDtypeStruct((bsz, seq, SSM_WIDTH), bf16),
        scratch_shapes=[pltpu.VMEM((N_SSM_GROUPS, D_STATE, GROUP_W), f32)],
        compiler_params=_cparams(2),
        name="ssd_bwd",
    )(act, dt3, y_part, xz3, dt_bias, a_log, norm_w, tril, eb)


def _ssd(xz3, dt3, conv_w, conv_b, dt_bias, a_log, d_skip, norm_w, consts):
    y_part, act = _ssd_fwd(xz3, dt3, conv_w, conv_b, dt_bias, a_log, d_skip, consts)
    return _ssd_bwd(act, dt3, y_part, xz3, dt_bias, a_log, norm_w, consts)


def _ssd_consts():
    idx = np.arange(CHUNK)
    tril = (idx[None, :] <= idx[:, None]).astype(np.float32)
    head = np.arange(SSM_WIDTH) // SSM_HEAD_DIM
    rows = np.arange(LANES)
    ef = (rows[:, None] == head[None, :]).astype(np.float32)
    eb = (rows[:, None] == head[None, :] + N_SSM_HEADS).astype(np.float32)
    return tuple(jnp.asarray(m, bf16) for m in (tril, tril.T, ef, eb))


def _out_proj_kernel(a_ref, s_ref, w_ref, x_ref, g_ref, o_ref):
    mix = jnp.dot(a_ref[...], w_ref[:ATTN_WIDTH, :], preferred_element_type=f32)
    mix = mix + jnp.dot(s_ref[...], w_ref[ATTN_WIDTH:, :], preferred_element_type=f32)
    o_ref[...] = x_ref[...] + _rms(mix, g_ref[...])


def _out_proj(a2, s2, w_out, x2, gain, tm):
    rows = x2.shape[0]
    return pl.pallas_call(
        _out_proj_kernel,
        grid=(rows // tm,),
        in_specs=[
            pl.BlockSpec((tm, ATTN_WIDTH), lambda i: (i, 0)),
            pl.BlockSpec((tm, SSM_WIDTH), lambda i: (i, 0)),
            pl.BlockSpec((D_MODEL, D_MODEL), lambda i: (0, 0)),
            pl.BlockSpec((tm, D_MODEL), lambda i: (i, 0)),
            pl.BlockSpec((1, D_MODEL), lambda i: (0, 0)),
        ],
        out_specs=pl.BlockSpec((tm, D_MODEL), lambda i: (i, 0)),
        out_shape=jax.ShapeDtypeStruct((rows, D_MODEL), f32),
        compiler_params=_cparams(1),
        name="out_proj",
    )(a2, s2, w_out, x2, gain)


def _mlp_kernel(x_ref, gpre_ref, wup_ref, wdn_ref, gpost_ref, o_ref, h_ref):
    j = pl.program_id(1)

    @pl.when(j == 0)
    def _():
        h_ref[...] = _rms(x_ref[...], gpre_ref[...]).astype(bf16)
        o_ref[...] = jnp.zeros_like(o_ref)

    u = jnp.dot(h_ref[...], wup_ref[...], preferred_element_type=f32)
    u = jnp.square(jnp.maximum(u, 0.0)).astype(bf16)
    o_ref[...] += jnp.dot(u, wdn_ref[...], preferred_element_type=f32)

    @pl.when(j == pl.num_programs(1) - 1)
    def _():
        o_ref[...] = x_ref[...] + _rms(o_ref[...], gpost_ref[...])


def _mlp(x2, g_pre, w_up, w_down, g_post, tm, tf):
    rows = x2.shape[0]
    return pl.pallas_call(
        _mlp_kernel,
        grid=(rows // tm, D_FF // tf),
        in_specs=[
            pl.BlockSpec((tm, D_MODEL), lambda i, j: (i, 0)),
            pl.BlockSpec((1, D_MODEL), lambda i, j: (0, 0)),
            pl.BlockSpec((D_MODEL, tf), lambda i, j: (0, j)),
            pl.BlockSpec((tf, D_MODEL), lambda i, j: (j, 0)),
            pl.BlockSpec((1, D_MODEL), lambda i, j: (0, 0)),
        ],
        out_specs=pl.BlockSpec((tm, D_MODEL), lambda i, j: (i, 0)),
        out_shape=jax.ShapeDtypeStruct((rows, D_MODEL), f32),
        scratch_shapes=[pltpu.VMEM((tm, D_MODEL), bf16)],
        compiler_params=_cparams(2),
        name="mlp",
    )(x2, g_pre, w_up, w_down, g_post)


def _attn_tiles(seq):
    for tq, tk in ((1024, 512), (512, 512), (256, 256)):
        if seq % tq == 0 and seq // tk >= tq // tk + 4:
            return tq, tk
    raise ValueError(f"sequence length {seq} is too short for the attention tiling")


def _trunk(x, layers, rel_bias, bias, consts, tq, tk):
    bsz, seq = x.shape[0], x.shape[1]
    rows = bsz * seq
    tm = tk
    x2 = x.reshape(rows, D_MODEL)
    for lp in layers:
        qk, vt, xz, dt = _in_proj(x2, lp["pre_norm_mix"], lp["w_main"], lp["w_dt"], bsz, seq, tm)
        a_out = _attention(qk.reshape(bsz, seq, 2 * ATTN_WIDTH), vt, bias, rel_bias, lp["lam"], lp["attn_norm"],
                           lp["lambda_init"], tq, tk)
        s_out = _ssd(xz.reshape(bsz, seq, CONV_CH + SSM_WIDTH), dt.reshape(bsz, seq, LANES),
                     lp["conv_w"], lp["conv_b"], lp["dt_bias"], lp["a_log"], lp["d_skip"], lp["ssm_norm"],
                     consts)
        x2 = _out_proj(a_out.reshape(rows, ATTN_WIDTH), s_out.reshape(rows, SSM_WIDTH), lp["w_out"], x2,
                       lp["post_norm_mix"], tm)
        x2 = _mlp(x2, lp["pre_norm_mlp"], lp["w_up"], lp["w_down"], lp["post_norm_mlp"], tm, 512)
    return x2.reshape(bsz, seq, D_MODEL)


def _pad_lanes(v, width=LANES):
    return jnp.pad(v, (0, width - v.shape[0]))[None, :].astype(f32)


def kernel(x_prompt, x_sample, rel_bias, pre_norm_mix, w_in, lambda_q1, lambda_k1, lambda_q2, lambda_k2,
           attn_norm, conv_w, conv_b, dt_bias_fwd, dt_bias_bwd, a_log_fwd, a_log_bwd, d_skip, ssm_norm,
           w_out, post_norm_mix, pre_norm_mlp, w_up, w_down, post_norm_mlp):
    depth = w_in.shape[0]
    row = lambda v: v[None, :].astype(f32)
    layers = []
    for i in range(depth):
        lambda_init = 0.8 - 0.6 * math.exp(-0.3 * i)
        lam = (jnp.exp(jnp.sum(lambda_q1[i].astype(f32) * lambda_k1[i].astype(f32)))
               - jnp.exp(jnp.sum(lambda_q2[i].astype(f32) * lambda_k2[i].astype(f32))) + lambda_init)
        wi = w_in[i]
        w_main = jnp.concatenate([wi[:, OFF_Q:OFF_Z], wi[:, OFF_XBC:OFF_DT], wi[:, OFF_Z:OFF_XBC]],
                                 axis=1).astype(bf16)
        w_dt = jnp.pad(wi[:, OFF_DT:], ((0, 0), (0, LANES - 2 * N_SSM_HEADS))).astype(bf16)
        layers.append(dict(
            lambda_init=lambda_init,
            lam=lam.reshape(1).astype(f32),
            pre_norm_mix=row(pre_norm_mix[i]), w_main=w_main, w_dt=w_dt,
            attn_norm=row(attn_norm[i]),
            conv_w=jnp.pad(conv_w[i].astype(f32), ((0, 8 - D_CONV), (0, 0))), conv_b=row(conv_b[i]),
            dt_bias=_pad_lanes(jnp.concatenate([dt_bias_fwd[i], dt_bias_bwd[i]])),
            a_log=_pad_lanes(jnp.concatenate([a_log_fwd[i], a_log_bwd[i]])),
            d_skip=row(jnp.repeat(d_skip[i], SSM_HEAD_DIM)), ssm_norm=row(ssm_norm[i]),
            w_out=w_out[i].astype(bf16), post_norm_mix=row(post_norm_mix[i]),
            pre_norm_mlp=row(pre_norm_mlp[i]), w_up=w_up[i].astype(bf16), w_down=w_down[i].astype(bf16),
            post_norm_mlp=row(post_norm_mlp[i]),
        ))
    consts = _ssd_consts()
    rel = rel_bias.astype(f32)
    outs = []
    biases = {}
    for x in (x_prompt, x_sample):
        tq, tk = _attn_tiles(x.shape[1])
        if (tq, tk) not in biases:
            biases[(tq, tk)] = _bias_tiles(rel, tq, tk)
        outs.append(_trunk(x, layers, rel, biases[(tq, tk)], consts, tq, tk))
    return tuple(outs)
```

```python
import functools
import math

import jax
import jax.numpy as jnp
import numpy as np
from jax import lax
from jax.experimental import pallas as pl
from jax.experimental.pallas import tpu as pltpu

f32 = jnp.float32
bf16 = jnp.bfloat16

D_MODEL = 2048
N_ATTN_HEADS = 8
ATTN_HEAD_DIM = 64
ATTN_V_DIM = 2 * ATTN_HEAD_DIM
ATTN_WIDTH = N_ATTN_HEADS * ATTN_V_DIM
SSM_WIDTH = 1024
SSM_HEAD_DIM = 64
N_SSM_HEADS = 16
N_SSM_GROUPS = 4
SSM_HPG = 4
D_STATE = 128
D_CONV = 5
CHUNK = 128
CONV_CH = SSM_WIDTH + 2 * N_SSM_GROUPS * D_STATE
GROUP_W = SSM_HPG * SSM_HEAD_DIM
OFF_Q, OFF_K, OFF_V, OFF_Z = 0, 1024, 2048, 3072
OFF_XBC = 4096
OFF_DT = OFF_XBC + CONV_CH
D_FF = 4 * D_MODEL
NUM_BUCKETS = 32
MAX_DISTANCE = 128
EPS = 1e-6
NEG = -1e30

LANES = 128
MXU_COLS = 256
HALO = 16
_SIDE_TAPS = tuple(j for j in range(D_CONV) if j != D_CONV // 2)
VMEM_LIMIT = 56 * 1024 * 1024


def _cparams(n_axes):
    return pltpu.CompilerParams(dimension_semantics=("arbitrary",) * n_axes,
                                vmem_limit_bytes=VMEM_LIMIT)


def _rms(xf, g):
    ms = jnp.mean(xf * xf, axis=-1, keepdims=True)
    return (xf * lax.rsqrt(ms + EPS)) * g


def _split3(v):
    hi = v.astype(bf16)
    r1 = v - hi.astype(f32)
    mid = r1.astype(bf16)
    lo = (r1 - mid.astype(f32)).astype(bf16)
    return hi, mid, lo


def _dot3(v, m01):
    hi, mid, lo = _split3(v)
    d = lambda a: jnp.dot(a, m01, preferred_element_type=f32)
    return d(hi) + d(mid) + d(lo)


def _expand_heads(v, e3, first):
    lane = lax.broadcasted_iota(jnp.int32, v.shape, 1)
    hi, mid, lo = _split3(jnp.where((lane >= first) & (lane < first + N_SSM_HEADS), v, 0.0))
    packed = (hi.astype(f32) + pltpu.roll(mid.astype(f32), N_SSM_HEADS, 1)
              + pltpu.roll(lo.astype(f32), 2 * N_SSM_HEADS, 1))
    return jnp.dot(packed.astype(bf16), e3, preferred_element_type=f32)


def _dot3_left(m01, v):
    hi, mid, lo = _split3(v)
    d = lambda a: jnp.dot(m01, a, preferred_element_type=f32)
    return d(hi) + d(mid) + d(lo)


def _in_proj_kernel(x_ref, g_ref, w_ref, wdt_ref, qk_ref, vt_ref, xz_ref, dt_ref, h_ref):
    j = pl.program_id(1)

    @pl.when(j == 0)
    def _():
        hb = _rms(x_ref[...], g_ref[...]).astype(bf16)
        h_ref[...] = hb
        dt_ref[...] = jnp.dot(hb, wdt_ref[...], preferred_element_type=f32)

    acc = jnp.dot(h_ref[...], w_ref[...], preferred_element_type=f32)

    @pl.when(j == 0)
    def _():
        qk_ref[...] = (acc * (ATTN_HEAD_DIM ** -0.5)).astype(bf16)

    @pl.when(j == 1)
    def _():
        qk_ref[...] = acc.astype(bf16)

    @pl.when(j == 2)
    def _():
        tm = acc.shape[0]
        vt_ref[0, :, 0] = acc.T.astype(bf16).reshape(N_ATTN_HEADS, ATTN_V_DIM, tm)

    @pl.when(j >= 3)
    def _():
        xz_ref[...] = acc.astype(bf16)


def _in_proj(x2, gain, w_main, w_dt, bsz, seq, tm):
    rows = x2.shape[0]
    nl = seq // tm
    nj = w_main.shape[1] // ATTN_WIDTH
    return pl.pallas_call(
        _in_proj_kernel,
        grid=(rows // tm, nj),
        in_specs=[
            pl.BlockSpec((tm, D_MODEL), lambda i, j: (i, 0)),
            pl.BlockSpec((1, D_MODEL), lambda i, j: (0, 0)),
            pl.BlockSpec((D_MODEL, ATTN_WIDTH), lambda i, j: (0, j)),
            pl.BlockSpec((D_MODEL, LANES), lambda i, j: (0, 0)),
        ],
        out_specs=[
            pl.BlockSpec((tm, ATTN_WIDTH), lambda i, j: (i, jnp.minimum(j, 1))),
            pl.BlockSpec((1, N_ATTN_HEADS, 1, ATTN_V_DIM, tm), lambda i, j: (i // nl, 0, i % nl, 0, 0)),
            pl.BlockSpec((tm, ATTN_WIDTH), lambda i, j: (i, jnp.clip(j - 3, 0, 2))),
            pl.BlockSpec((tm, LANES), lambda i, j: (i, 0)),
        ],
        out_shape=[
            jax.ShapeDtypeStruct((rows, 2 * ATTN_WIDTH), bf16),
            jax.ShapeDtypeStruct((bsz, N_ATTN_HEADS, nl, ATTN_V_DIM, tm), bf16),
            jax.ShapeDtypeStruct((rows, CONV_CH + SSM_WIDTH), bf16),
            jax.ShapeDtypeStruct((rows, LANES), f32),
        ],
        scratch_shapes=[pltpu.VMEM((tm, D_MODEL), bf16)],
        compiler_params=_cparams(2),
        name="in_proj",
    )(x2, gain, w_main, w_dt)


def _bias_kernel(tab_ref, o_ref, *, tq, tk):
    h = pl.program_id(0)
    d = pl.program_id(1) - 1
    kk = lax.broadcasted_iota(jnp.int32, (tk, tq), 0)
    qq = lax.broadcasted_iota(jnp.int32, (tk, tq), 1)
    rel = d * tk + kk - qq
    nb = NUM_BUCKETS // 2
    max_exact = nb // 2
    ret = jnp.where(rel > 0, nb, 0)
    n = jnp.abs(rel)
    nf = jnp.maximum(n, 1).astype(f32)
    large = max_exact + (jnp.log(nf / max_exact) / math.log(MAX_DISTANCE / max_exact)
                         * (nb - max_exact)).astype(jnp.int32)
    large = jnp.minimum(large, nb - 1)
    bucket = ret + jnp.where(n < max_exact, n, large)
    out = jnp.zeros((tk, tq), f32)
    for b in range(NUM_BUCKETS):
        out = jnp.where(bucket == b, tab_ref[b, h], out)
    o_ref[0, 0] = out


def _bias_tiles(rel_bias, tq, tk):
    n_near = tq // tk + 2
    return pl.pallas_call(
        functools.partial(_bias_kernel, tq=tq, tk=tk),
        grid=(N_ATTN_HEADS, n_near),
        in_specs=[pl.BlockSpec(memory_space=pltpu.SMEM)],
        out_specs=pl.BlockSpec((1, 1, tk, tq), lambda h, d: (h, d, 0, 0)),
        out_shape=jax.ShapeDtypeStruct((N_ATTN_HEADS, n_near, tk, tq), f32),
        compiler_params=_cparams(2),
        name="bias_tiles",
    )(rel_bias)


def _attn_kernel(tab_ref, lam_ref, q_ref, k_ref, vt_ref, bias_ref, w_ref, o_ref,
                 qcat_ref, s_ref, p_ref, m_ref, acc_ref, *, tq, tk, nk, out_scale):
    h = pl.program_id(1)
    qi = pl.program_id(2)
    r = tq // tk
    c_left = tab_ref[NUM_BUCKETS // 2 - 1, h]
    c_right = tab_ref[NUM_BUCKETS - 1, h]
    lam = lam_ref[0]

    qt = q_ref[0].astype(f32).T
    row = lax.broadcasted_iota(jnp.int32, qt.shape, 0)
    qcat_ref[...] = jnp.concatenate([jnp.where(row < ATTN_HEAD_DIM, qt, 0.0),
                                     jnp.where(row >= ATTN_HEAD_DIM, qt, 0.0)], axis=1).astype(bf16)
    ones = jnp.ones((8, tk), bf16)
    near_lo = jnp.maximum(qi * r - 1, 0)
    near_hi = jnp.minimum(qi * r + r + 1, nk)
    n_far = nk - (near_hi - near_lo)

    def logits(kt):
        return jnp.dot(k_ref[0, kt], qcat_ref[...], preferred_element_type=f32)

    def pv(kt, p):
        vext = jnp.concatenate([vt_ref[0, 0, kt], ones], axis=0)
        return jnp.dot(vext, p, preferred_element_type=f32)

    def far_kt(f):
        return f + jnp.where(f >= near_lo, near_hi - near_lo, 0)

    def far_c(f):
        return jnp.where(f >= near_lo, c_right, c_left)

    def stage_a(f):
        s = logits(far_kt(f))
        s_ref[...] = s
        return jnp.max(s, axis=0, keepdims=True)

    def stage_b(f, smax):
        c = far_c(f)
        m = m_ref[...]
        m_new = jnp.maximum(m, smax + c)
        p_ref[...] = jnp.exp(s_ref[...] - (m_new - c)).astype(bf16)
        m_ref[...] = m_new
        return jnp.exp(m - m_new)

    def stage_c(f, alpha):
        acc_ref[...] = acc_ref[...] * alpha + pv(far_kt(f), p_ref[...])

    m_ref[...] = jnp.full(m_ref.shape, NEG, f32)
    acc_ref[...] = jnp.zeros(acc_ref.shape, f32)

    smax = stage_a(0)
    alpha = stage_b(0, smax)
    smax = stage_a(1)

    def far_body(i, carry):
        alpha, smax = carry
        stage_c(i - 2, alpha)
        alpha = stage_b(i - 1, smax)
        smax = stage_a(i)
        return alpha, smax

    alpha, smax = lax.fori_loop(2, n_far, far_body, (alpha, smax))
    stage_c(n_far - 2, alpha)
    alpha = stage_b(n_far - 1, smax)
    stage_c(n_far - 1, alpha)

    def near(kt, carry):
        b = bias_ref[0, kt - qi * r + 1]
        s = logits(kt) + jnp.concatenate([b, b], axis=1)
        m = m_ref[...]
        m_new = jnp.maximum(m, jnp.max(s, axis=0, keepdims=True))
        p = jnp.exp(s - m_new).astype(bf16)
        acc_ref[...] = acc_ref[...] * jnp.exp(m - m_new) + pv(kt, p)
        m_ref[...] = m_new
        return carry

    lax.fori_loop(near_lo, near_hi, near, 0)

    acc = acc_ref[...]
    o = acc[:ATTN_V_DIM] * (1.0 / acc[ATTN_V_DIM:ATTN_V_DIM + 1])
    diff = (o[:, :tq] - lam * o[:, tq:]).T
    o_ref[0] = (_rms(diff, w_ref[...]) * out_scale).astype(bf16)


def _attention(qk3, vt, bias, rel_bias, lam, subln_w, lambda_init, tq, tk):
    bsz, seq = qk3.shape[0], qk3.shape[1]
    nk = seq // tk
    assert nk >= tq // tk + 4, "the far-tile pipeline needs at least two far key tiles per query tile"
    qk4 = qk3.reshape(bsz, nk, tk, 2 * ATTN_WIDTH)
    kern = functools.partial(_attn_kernel, tq=tq, tk=tk, nk=nk, out_scale=1.0 - lambda_init)
    return pl.pallas_call(
        kern,
        grid=(bsz, N_ATTN_HEADS, seq // tq),
        in_specs=[
            pl.BlockSpec(memory_space=pltpu.SMEM),
            pl.BlockSpec(memory_space=pltpu.SMEM),
            pl.BlockSpec((1, tq, ATTN_V_DIM), lambda b, h, i: (b, i, h)),
            pl.BlockSpec((1, nk, tk, ATTN_V_DIM), lambda b, h, i: (b, 0, 0, N_ATTN_HEADS + h)),
            pl.BlockSpec((1, 1, nk, ATTN_V_DIM, tk), lambda b, h, i: (b, h, 0, 0, 0)),
            pl.BlockSpec((1, tq // tk + 2, tk, tq), lambda b, h, i: (h, 0, 0, 0)),
            pl.BlockSpec((1, ATTN_V_DIM), lambda b, h, i: (0, 0)),
        ],
        out_specs=pl.BlockSpec((1, tq, ATTN_V_DIM), lambda b, h, i: (b, i, h)),
        out_shape=jax.ShapeDtypeStruct((bsz, seq, ATTN_WIDTH), bf16),
        scratch_shapes=[
            pltpu.VMEM((ATTN_V_DIM, 2 * tq), bf16),
            pltpu.VMEM((tk, 2 * tq), f32),
            pltpu.VMEM((tk, 2 * tq), bf16),
            pltpu.VMEM((1, 2 * tq), f32),
            pltpu.VMEM((ATTN_V_DIM + 8, 2 * tq), f32),
        ],
        compiler_params=_cparams(3),
        name="attention",
    )(rel_bias, lam, qk3, qk4, vt, bias, subln_w)


def _softplus(v):
    return jnp.maximum(v, 0.0) + jnp.log1p(jnp.exp(-jnp.abs(v)))


def _ssd_decay_terms(dt_ref, dtb_ref, alog_ref, tril_ref):
    dtv = _softplus(dt_ref[0] + dtb_ref[...])
    lane = lax.broadcasted_iota(jnp.int32, (1, LANES), 1)
    a_row = jnp.where(lane < 2 * N_SSM_HEADS, -jnp.exp(alog_ref[...]), 0.0)
    a = dtv * a_row
    cs = _dot3_left(tril_ref[...], a)
    return dtv, a, cs


def _ssd_fwd_kernel(cur_ref, prev_ref, next_ref, dt_ref, cw_ref, cb_ref, dtb_ref, alog_ref, dskip_ref,
                    tril_ref, triu_ref, ef_ref, shift_ref, y_ref, act_ref, state_ref):
    c = pl.program_id(1)
    nc = pl.num_programs(1)

    @pl.when(c == 0)
    def _():
        state_ref[...] = jnp.zeros_like(state_ref)

    pmask = (c > 0).astype(f32)
    nmask = (c < nc - 1).astype(f32)
    ext = jnp.concatenate([prev_ref[0] * pmask.astype(bf16), cur_ref[0], next_ref[0] * nmask.astype(bf16)],
                          axis=0)
    taps = jnp.dot(shift_ref[...], ext, preferred_element_type=f32)
    mid = D_CONV // 2
    conv = cur_ref[0].astype(f32) * cw_ref[mid:mid + 1, :] + cb_ref[...]
    for i, j in enumerate(_SIDE_TAPS):
        conv = conv + taps[i * CHUNK:(i + 1) * CHUNK, :] * cw_ref[j:j + 1, :]
    act = conv * (1.0 / (1.0 + jnp.exp(-conv)))
    actb = act.astype(bf16)
    act_ref[0] = actb

    dtv, a, cs = _ssd_decay_terms(dt_ref, dtb_ref, alog_ref, tril_ref)
    a_t = a.T
    dt_t = dtv.T
    cs_t = _dot3(a_t, triu_ref[...])
    ecs = cs - a
    ecs_t = cs_t - a_t

    tot = cs[CHUNK - 1:CHUNK, :]
    scale_off = _expand_heads(jnp.exp(cs), ef_ref[...], 0)
    scale_in = _expand_heads(dtv * jnp.exp(tot - cs), ef_ref[...], 0)
    carry_dec = scale_off[CHUNK - 1:CHUNK, :]

    li = lax.broadcasted_iota(jnp.int32, (CHUNK, CHUNK), 0)
    si = lax.broadcasted_iota(jnp.int32, (CHUNK, CHUNK), 1)
    lower = li >= si
    upper = si >= li

    for g in range(N_SSM_GROUPS):
        gs = slice(g * GROUP_W, (g + 1) * GROUP_W)
        bg = actb[:, SSM_WIDTH + g * D_STATE: SSM_WIDTH + (g + 1) * D_STATE]
        cg = actb[:, SSM_WIDTH + N_SSM_GROUPS * D_STATE + g * D_STATE:
                  SSM_WIDTH + N_SSM_GROUPS * D_STATE + (g + 1) * D_STATE]
        cbm = lax.dot_general(cg, bg, (((1,), (1,)), ((), ())), preferred_element_type=f32)
        ws, xs_blocks = [], []
        for hh in range(SSM_HPG):
            hd = g * SSM_HPG + hh
            hb = N_SSM_HEADS + hd
            dec_f = jnp.exp(jnp.where(lower, cs[:, hd:hd + 1] - cs_t[hd:hd + 1, :], NEG))
            dec_b = jnp.exp(jnp.where(upper, ecs_t[hb:hb + 1, :] - ecs[:, hb:hb + 1], NEG))
            w = cbm * (dec_f * dt_t[hd:hd + 1, :] + dec_b * dt_t[hb:hb + 1, :])
            ws.append(w.astype(bf16))
            lane = lax.broadcasted_iota(jnp.int32, (CHUNK, GROUP_W), 1)
            own = (lane >= hh * SSM_HEAD_DIM) & (lane < (hh + 1) * SSM_HEAD_DIM)
            xs_blocks.append(jnp.where(own, actb[:, gs], jnp.zeros((), bf16)))
        xg = act[:, gs]
        wcat = jnp.concatenate(ws, axis=1)
        xblk = jnp.concatenate(xs_blocks, axis=0)
        y = jnp.dot(wcat, xblk, preferred_element_type=f32)
        y = y + jnp.dot(cg, state_ref[g].astype(bf16), preferred_element_type=f32) * scale_off[:, gs]
        y_ref[0, :, gs] = y + xg * dskip_ref[:, gs]
        xs = (xg * scale_in[:, gs]).astype(bf16)
        new = jnp.dot(bg.astype(f32).T.astype(bf16), xs, preferred_element_type=f32)
        state_ref[g] = state_ref[g] * carry_dec[:, gs] + new


def _ssd_bwd_kernel(act_ref, dt_ref, y_ref, z_ref, dtb_ref, alog_ref, nw_ref, tril_ref, eb_ref,
                    o_ref, state_ref):
    c = pl.program_id(1)

    @pl.when(c == 0)
    def _():
        state_ref[...] = jnp.zeros_like(state_ref)

    actb = act_ref[0]
    dtv, a, cs = _ssd_decay_terms(dt_ref, dtb_ref, alog_ref, tril_ref)
    ecs = cs - a
    tot = cs[CHUNK - 1:CHUNK, :]
    scale_off = _expand_heads(jnp.exp(tot - ecs), eb_ref[...], N_SSM_HEADS)
    scale_in = _expand_heads(dtv * jnp.exp(ecs), eb_ref[...], N_SSM_HEADS)
    carry_dec = scale_off[0:1, :]

    zf = z_ref[0].astype(f32)
    gate = zf * (1.0 / (1.0 + jnp.exp(-zf)))
    for g in range(N_SSM_GROUPS):
        gs = slice(g * GROUP_W, (g + 1) * GROUP_W)
        bg = actb[:, SSM_WIDTH + g * D_STATE: SSM_WIDTH + (g + 1) * D_STATE]
        cg = actb[:, SSM_WIDTH + N_SSM_GROUPS * D_STATE + g * D_STATE:
                  SSM_WIDTH + N_SSM_GROUPS * D_STATE + (g + 1) * D_STATE]
        y = y_ref[0, :, gs] + (jnp.dot(cg, state_ref[g].astype(bf16), preferred_element_type=f32)
                               * scale_off[:, gs])
        xs = (actb[:, gs].astype(f32) * scale_in[:, gs]).astype(bf16)
        new = jnp.dot(bg.astype(f32).T.astype(bf16), xs, preferred_element_type=f32)
        state_ref[g] = state_ref[g] * carry_dec[:, gs] + new
        o_ref[0, :, gs] = _rms(y * gate[:, gs], nw_ref[:, gs]).astype(bf16)


def _const_spec(shape):
    return pl.BlockSpec(shape, lambda b, c: (0,) * len(shape))


def _ssd_fwd(xz3, dt3, conv_w, conv_b, dt_bias, a_log, d_skip, consts):
    bsz, seq = xz3.shape[0], xz3.shape[1]
    nc = seq // CHUNK
    hb = CHUNK // HALO
    nh = seq // HALO
    tril, triu, ef, _, shift = consts
    const = _const_spec
    return pl.pallas_call(
        _ssd_fwd_kernel,
        grid=(bsz, nc),
        in_specs=[
            pl.BlockSpec((1, CHUNK, CONV_CH), lambda b, c: (b, c, 0)),
            pl.BlockSpec((1, HALO, CONV_CH), lambda b, c: (b, jnp.maximum(c * hb - 1, 0), 0)),
            pl.BlockSpec((1, HALO, CONV_CH), lambda b, c: (b, jnp.minimum((c + 1) * hb, nh - 1), 0)),
            pl.BlockSpec((1, CHUNK, LANES), lambda b, c: (b, c, 0)),
            const((8, CONV_CH)), const((1, CONV_CH)), const((1, LANES)), const((1, LANES)),
            const((1, SSM_WIDTH)), const((CHUNK, CHUNK)), const((CHUNK, CHUNK)), const((LANES, SSM_WIDTH)),
            const(((D_CONV - 1) * CHUNK, CHUNK + 2 * HALO)),
        ],
        out_specs=[
            pl.BlockSpec((1, CHUNK, SSM_WIDTH), lambda b, c: (b, c, 0)),
            pl.BlockSpec((1, CHUNK, CONV_CH), lambda b, c: (b, c, 0)),
        ],
        out_shape=[
            jax.ShapeDtypeStruct((bsz, seq, SSM_WIDTH), f32),
            jax.ShapeDtypeStruct((bsz, seq, CONV_CH), bf16),
        ],
        scratch_shapes=[pltpu.VMEM((N_SSM_GROUPS, D_STATE, GROUP_W), f32)],
        compiler_params=_cparams(2),
        name="ssd_fwd",
    )(xz3, xz3, xz3, dt3, conv_w, conv_b, dt_bias, a_log, d_skip, tril, triu, ef, shift)


def _ssd_bwd(act, dt3, y_part, xz3, dt_bias, a_log, norm_w, consts):
    bsz, seq = xz3.shape[0], xz3.shape[1]
    nc = seq // CHUNK
    tril, _, _, eb, _ = consts
    const = _const_spec
    rev = lambda b, c: (b, nc - 1 - c, 0)
    return pl.pallas_call(
        _ssd_bwd_kernel,
        grid=(bsz, nc),
        in_specs=[
            pl.BlockSpec((1, CHUNK, CONV_CH), rev),
            pl.BlockSpec((1, CHUNK, LANES), rev),
            pl.BlockSpec((1, CHUNK, SSM_WIDTH), rev),
            pl.BlockSpec((1, CHUNK, SSM_WIDTH), lambda b, c: (b, nc - 1 - c, CONV_CH // SSM_WIDTH)),
            const((1, LANES)), const((1, LANES)), const((1, SSM_WIDTH)),
            const((CHUNK, CHUNK)), const((LANES, SSM_WIDTH)),
        ],
        out_specs=pl.BlockSpec((1, CHUNK, SSM_WIDTH), rev),
        out_shape=jax.ShapeDtypeStruct((bsz, seq, SSM_WIDTH), bf16),
        scratch_shapes=[pltpu.VMEM((N_SSM_GROUPS, D_STATE, GROUP_W), f32)],
        compiler_params=_cparams(2),
        name="ssd_bwd",
    )(act, dt3, y_part, xz3, dt_bias, a_log, norm_w, tril, eb)


def _ssd(xz3, dt3, conv_w, conv_b, dt_bias, a_log, d_skip, norm_w, consts):
    y_part, act = _ssd_fwd(xz3, dt3, conv_w, conv_b, dt_bias, a_log, d_skip, consts)
    return _ssd_bwd(act, dt3, y_part, xz3, dt_bias, a_log, norm_w, consts)


def _ssd_consts():
    idx = np.arange(CHUNK)
    tril = (idx[None, :] <= idx[:, None]).astype(np.float32)
    head = np.arange(SSM_WIDTH) // SSM_HEAD_DIM
    rows = np.arange(LANES)
    piece = lambda first: (rows >= first) & (rows < first + 3 * N_SSM_HEADS)
    ef = (piece(0)[:, None] & (rows[:, None] % N_SSM_HEADS == head[None, :])).astype(np.float32)
    eb = (piece(N_SSM_HEADS)[:, None] & (rows[:, None] % N_SSM_HEADS == head[None, :])).astype(np.float32)
    out_row = np.arange((D_CONV - 1) * CHUNK)
    tap = np.asarray(_SIDE_TAPS)[out_row // CHUNK]
    src = HALO + out_row % CHUNK + tap - D_CONV // 2
    shift = (src[:, None] == np.arange(CHUNK + 2 * HALO)[None, :]).astype(np.float32)
    return tuple(jnp.asarray(m, bf16) for m in (tril, tril.T, ef, eb, shift))


def _out_proj_kernel(a_ref, s_ref, w_ref, x_ref, g_ref, o_ref):
    mix = jnp.dot(a_ref[...], w_ref[:ATTN_WIDTH, :], preferred_element_type=f32)
    mix = mix + jnp.dot(s_ref[...], w_ref[ATTN_WIDTH:, :], preferred_element_type=f32)
    o_ref[...] = x_ref[...] + _rms(mix, g_ref[...])


def _out_proj(a2, s2, w_out, x2, gain, tm):
    rows = x2.shape[0]
    return pl.pallas_call(
        _out_proj_kernel,
        grid=(rows // tm,),
        in_specs=[
            pl.BlockSpec((tm, ATTN_WIDTH), lambda i: (i, 0)),
            pl.BlockSpec((tm, SSM_WIDTH), lambda i: (i, 0)),
            pl.BlockSpec((D_MODEL, D_MODEL), lambda i: (0, 0)),
            pl.BlockSpec((tm, D_MODEL), lambda i: (i, 0)),
            pl.BlockSpec((1, D_MODEL), lambda i: (0, 0)),
        ],
        out_specs=pl.BlockSpec((tm, D_MODEL), lambda i: (i, 0)),
        out_shape=jax.ShapeDtypeStruct((rows, D_MODEL), f32),
        compiler_params=_cparams(1),
        name="out_proj",
    )(a2, s2, w_out, x2, gain)


def _mlp_kernel(x_ref, gpre_ref, wup_ref, wdn_ref, gpost_ref, o_ref, h_ref):
    j = pl.program_id(1)

    @pl.when(j == 0)
    def _():
        h_ref[...] = _rms(x_ref[...], gpre_ref[...]).astype(bf16)
        o_ref[...] = jnp.zeros_like(o_ref)

    u = jnp.dot(h_ref[...], wup_ref[...], preferred_element_type=f32)
    u = jnp.square(jnp.maximum(u, 0.0)).astype(bf16)
    o_ref[...] += jnp.dot(u, wdn_ref[...], preferred_element_type=f32)

    @pl.when(j == pl.num_programs(1) - 1)
    def _():
        o_ref[...] = x_ref[...] + _rms(o_ref[...], gpost_ref[...])


def _mlp(x2, g_pre, w_up, w_down, g_post, tm, tf):
    rows = x2.shape[0]
    return pl.pallas_call(
        _mlp_kernel,
        grid=(rows // tm, D_FF // tf),
        in_specs=[
            pl.BlockSpec((tm, D_MODEL), lambda i, j: (i, 0)),
            pl.BlockSpec((1, D_MODEL), lambda i, j: (0, 0)),
            pl.BlockSpec((D_MODEL, tf), lambda i, j: (0, j)),
            pl.BlockSpec((tf, D_MODEL), lambda i, j: (j, 0)),
            pl.BlockSpec((1, D_MODEL), lambda i, j: (0, 0)),
        ],
        out_specs=pl.BlockSpec((tm, D_MODEL), lambda i, j: (i, 0)),
        out_shape=jax.ShapeDtypeStruct((rows, D_MODEL), f32),
        scratch_shapes=[pltpu.VMEM((tm, D_MODEL), bf16)],
        compiler_params=_cparams(2),
        name="mlp",
    )(x2, g_pre, w_up, w_down, g_post)


def _attn_tiles(seq):
    for tq, tk in ((1024, 512), (512, 512), (256, 256)):
        if seq % tq == 0 and seq // tk >= tq // tk + 4:
            return tq, tk
    raise ValueError(f"sequence length {seq} is too short for the attention tiling")


def _trunk(x, layers, rel_bias, bias, consts, tq, tk):
    bsz, seq = x.shape[0], x.shape[1]
    rows = bsz * seq
    tm = tk
    x2 = x.reshape(rows, D_MODEL)
    for lp in layers:
        qk, vt, xz, dt = _in_proj(x2, lp["pre_norm_mix"], lp["w_main"], lp["w_dt"], bsz, seq, tm)
        a_out = _attention(qk.reshape(bsz, seq, 2 * ATTN_WIDTH), vt, bias, rel_bias, lp["lam"], lp["attn_norm"],
                           lp["lambda_init"], tq, tk)
        s_out = _ssd(xz.reshape(bsz, seq, CONV_CH + SSM_WIDTH), dt.reshape(bsz, seq, LANES),
                     lp["conv_w"], lp["conv_b"], lp["dt_bias"], lp["a_log"], lp["d_skip"], lp["ssm_norm"],
                     consts)
        x2 = _out_proj(a_out.reshape(rows, ATTN_WIDTH), s_out.reshape(rows, SSM_WIDTH), lp["w_out"], x2,
                       lp["post_norm_mix"], tm)
        tm_mlp = 1024 if rows % 1024 == 0 else tm
        x2 = _mlp(x2, lp["pre_norm_mlp"], lp["w_up"], lp["w_down"], lp["post_norm_mlp"], tm_mlp, 512)
    return x2.reshape(bsz, seq, D_MODEL)


def _pad_lanes(v, width=LANES):
    return jnp.pad(v, (0, width - v.shape[0]))[None, :].astype(f32)


def kernel(x_prompt, x_sample, rel_bias, pre_norm_mix, w_in, lambda_q1, lambda_k1, lambda_q2, lambda_k2,
           attn_norm, conv_w, conv_b, dt_bias_fwd, dt_bias_bwd, a_log_fwd, a_log_bwd, d_skip, ssm_norm,
           w_out, post_norm_mix, pre_norm_mlp, w_up, w_down, post_norm_mlp):
    depth = w_in.shape[0]
    row = lambda v: v[None, :].astype(f32)
    layers = []
    for i in range(depth):
        lambda_init = 0.8 - 0.6 * math.exp(-0.3 * i)
        lam = (jnp.exp(jnp.sum(lambda_q1[i].astype(f32) * lambda_k1[i].astype(f32)))
               - jnp.exp(jnp.sum(lambda_q2[i].astype(f32) * lambda_k2[i].astype(f32))) + lambda_init)
        wi = w_in[i]
        w_main = jnp.concatenate([wi[:, OFF_Q:OFF_Z], wi[:, OFF_XBC:OFF_DT], wi[:, OFF_Z:OFF_XBC]],
                                 axis=1).astype(bf16)
        w_dt = jnp.pad(wi[:, OFF_DT:], ((0, 0), (0, LANES - 2 * N_SSM_HEADS))).astype(bf16)
        layers.append(dict(
            lambda_init=lambda_init,
            lam=lam.reshape(1).astype(f32),
            pre_norm_mix=row(pre_norm_mix[i]), w_main=w_main, w_dt=w_dt,
            attn_norm=row(attn_norm[i]),
            conv_w=jnp.pad(conv_w[i].astype(f32), ((0, 8 - D_CONV), (0, 0))), conv_b=row(conv_b[i]),
            dt_bias=_pad_lanes(jnp.concatenate([dt_bias_fwd[i], dt_bias_bwd[i]])),
            a_log=_pad_lanes(jnp.concatenate([a_log_fwd[i], a_log_bwd[i]])),
            d_skip=row(jnp.repeat(d_skip[i], SSM_HEAD_DIM)), ssm_norm=row(ssm_norm[i]),
            w_out=w_out[i].astype(bf16), post_norm_mix=row(post_norm_mix[i]),
            pre_norm_mlp=row(pre_norm_mlp[i]), w_up=w_up[i].astype(bf16), w_down=w_down[i].astype(bf16),
            post_norm_mlp=row(post_norm_mlp[i]),
        ))
    consts = _ssd_consts()
    rel = rel_bias.astype(f32)
    outs = []
    biases = {}
    for x in (x_prompt, x_sample):
        tq, tk = _attn_tiles(x.shape[1])
        if (tq, tk) not in biases:
            biases[(tq, tk)] = _bias_tiles(rel, tq, tk)
        outs.append(_trunk(x, layers, rel, biases[(tq, tk)], consts, tq, tk))
    return tuple(outs)
```

```python
import functools
import math

import jax
import jax.numpy as jnp
import numpy as np
from jax import lax
from jax.experimental import pallas as pl
from jax.experimental.pallas import tpu as pltpu

f32 = jnp.float32
bf16 = jnp.bfloat16

D_MODEL = 2048
N_ATTN_HEADS = 8
ATTN_HEAD_DIM = 64
ATTN_V_DIM = 2 * ATTN_HEAD_DIM
ATTN_WIDTH = N_ATTN_HEADS * ATTN_V_DIM
SSM_WIDTH = 1024
SSM_HEAD_DIM = 64
N_SSM_HEADS = 16
N_SSM_GROUPS = 4
SSM_HPG = 4
D_STATE = 128
D_CONV = 5
CHUNK = 128
CONV_CH = SSM_WIDTH + 2 * N_SSM_GROUPS * D_STATE
GROUP_W = SSM_HPG * SSM_HEAD_DIM
OFF_Q, OFF_K, OFF_V, OFF_Z = 0, 1024, 2048, 3072
OFF_XBC = 4096
OFF_DT = OFF_XBC + CONV_CH
D_FF = 4 * D_MODEL
NUM_BUCKETS = 32
MAX_DISTANCE = 128
EPS = 1e-6
NEG = -1e30

LANES = 128
MXU_COLS = 256
HALO = 16
_SIDE_TAPS = tuple(j for j in range(D_CONV) if j != D_CONV // 2)
VMEM_LIMIT = 56 * 1024 * 1024


def _cparams(n_axes):
    return pltpu.CompilerParams(dimension_semantics=("arbitrary",) * n_axes,
                                vmem_limit_bytes=VMEM_LIMIT)


def _rms(xf, g):
    ms = jnp.mean(xf * xf, axis=-1, keepdims=True)
    return (xf * lax.rsqrt(ms + EPS)) * g


def _split3(v):
    hi = v.astype(bf16)
    r1 = v - hi.astype(f32)
    mid = r1.astype(bf16)
    lo = (r1 - mid.astype(f32)).astype(bf16)
    return hi, mid, lo


def _dot3(v, m01):
    hi, mid, lo = _split3(v)
    d = lambda a: jnp.dot(a, m01, preferred_element_type=f32)
    return d(hi) + d(mid) + d(lo)


def _expand_heads(v, e3, first):
    lane = lax.broadcasted_iota(jnp.int32, v.shape, 1)
    hi, mid, lo = _split3(jnp.where((lane >= first) & (lane < first + N_SSM_HEADS), v, 0.0))
    packed = (hi.astype(f32) + pltpu.roll(mid.astype(f32), N_SSM_HEADS, 1)
              + pltpu.roll(lo.astype(f32), 2 * N_SSM_HEADS, 1))
    return jnp.dot(packed.astype(bf16), e3, preferred_element_type=f32)


def _dot3_left(m01, v):
    hi, mid, lo = _split3(v)
    d = lambda a: jnp.dot(m01, a, preferred_element_type=f32)
    return d(hi) + d(mid) + d(lo)


def _in_proj_kernel(x_ref, g_ref, w_ref, wdt_ref, qk_ref, vt_ref, xz_ref, dt_ref, h_ref):
    j = pl.program_id(1)

    @pl.when(j == 0)
    def _():
        hb = _rms(x_ref[...], g_ref[...]).astype(bf16)
        h_ref[...] = hb
        dt_ref[...] = jnp.dot(hb, wdt_ref[...], preferred_element_type=f32)

    acc = jnp.dot(h_ref[...], w_ref[...], preferred_element_type=f32)

    @pl.when(j == 0)
    def _():
        qk_ref[...] = (acc * (ATTN_HEAD_DIM ** -0.5)).astype(bf16)

    @pl.when(j == 1)
    def _():
        qk_ref[...] = acc.astype(bf16)

    @pl.when(j == 2)
    def _():
        vt = acc.T.astype(bf16)
        tk = vt_ref.shape[-1]
        for u in range(vt_ref.shape[2]):
            vt_ref[0, :, u] = vt[:, u * tk:(u + 1) * tk].reshape(N_ATTN_HEADS, ATTN_V_DIM, tk)

    @pl.when(j >= 3)
    def _():
        xz_ref[...] = acc.astype(bf16)


def _in_proj(x2, gain, w_main, w_dt, bsz, seq, tm, tk):
    rows = x2.shape[0]
    nl = seq // tm
    nj = w_main.shape[1] // ATTN_WIDTH
    return pl.pallas_call(
        _in_proj_kernel,
        grid=(rows // tm, nj),
        in_specs=[
            pl.BlockSpec((tm, D_MODEL), lambda i, j: (i, 0)),
            pl.BlockSpec((1, D_MODEL), lambda i, j: (0, 0)),
            pl.BlockSpec((D_MODEL, ATTN_WIDTH), lambda i, j: (0, j)),
            pl.BlockSpec((D_MODEL, LANES), lambda i, j: (0, 0)),
        ],
        out_specs=[
            pl.BlockSpec((tm, ATTN_WIDTH), lambda i, j: (i, jnp.minimum(j, 1))),
            pl.BlockSpec((1, N_ATTN_HEADS, tm // tk, ATTN_V_DIM, tk), lambda i, j: (i // nl, 0, i % nl, 0, 0)),
            pl.BlockSpec((tm, ATTN_WIDTH), lambda i, j: (i, jnp.clip(j - 3, 0, 2))),
            pl.BlockSpec((tm, LANES), lambda i, j: (i, 0)),
        ],
        out_shape=[
            jax.ShapeDtypeStruct((rows, 2 * ATTN_WIDTH), bf16),
            jax.ShapeDtypeStruct((bsz, N_ATTN_HEADS, seq // tk, ATTN_V_DIM, tk), bf16),
            jax.ShapeDtypeStruct((rows, CONV_CH + SSM_WIDTH), bf16),
            jax.ShapeDtypeStruct((rows, LANES), f32),
        ],
        scratch_shapes=[pltpu.VMEM((tm, D_MODEL), bf16)],
        compiler_params=_cparams(2),
        name="in_proj",
    )(x2, gain, w_main, w_dt)


def _bias_block(tab_ref, h, delta):
    kk = lax.broadcasted_iota(jnp.int32, (MAX_DISTANCE, MAX_DISTANCE), 0)
    qq = lax.broadcasted_iota(jnp.int32, (MAX_DISTANCE, MAX_DISTANCE), 1)
    rel = delta * MAX_DISTANCE + kk - qq
    nb = NUM_BUCKETS // 2
    max_exact = nb // 2
    ret = jnp.where(rel > 0, nb, 0)
    n = jnp.abs(rel)
    nf = jnp.maximum(n, 1).astype(f32)
    large = max_exact + (jnp.log(nf / max_exact) / math.log(MAX_DISTANCE / max_exact)
                         * (nb - max_exact)).astype(jnp.int32)
    large = jnp.minimum(large, nb - 1)
    bucket = ret + jnp.where(n < max_exact, n, large)
    out = jnp.zeros(rel.shape, f32)
    for b in range(NUM_BUCKETS):
        out = jnp.where(bucket == b, tab_ref[b, h], out)
    return out


def _bias_kernel(tab_ref, o_ref, *, tq, tk):
    h = pl.program_id(0)
    blk = MAX_DISTANCE
    side = {-2: jnp.full((blk, blk), tab_ref[NUM_BUCKETS // 2 - 1, h], f32),
            2: jnp.full((blk, blk), tab_ref[NUM_BUCKETS - 1, h], f32)}
    band = {delta: _bias_block(tab_ref, h, delta) for delta in (-1, 0, 1)}
    for d in range(tq // tk + 2):
        for a in range(tk // blk):
            for b in range(tq // blk):
                delta = (d - 1) * (tk // blk) + a - b
                val = band[delta] if abs(delta) <= 1 else side[2 if delta > 0 else -2]
                o_ref[0, d, a * blk:(a + 1) * blk, b * blk:(b + 1) * blk] = val


def _bias_tiles(rel_bias, tq, tk):
    n_near = tq // tk + 2
    return pl.pallas_call(
        functools.partial(_bias_kernel, tq=tq, tk=tk),
        grid=(N_ATTN_HEADS,),
        in_specs=[pl.BlockSpec(memory_space=pltpu.SMEM)],
        out_specs=pl.BlockSpec((1, n_near, tk, tq), lambda h: (h, 0, 0, 0)),
        out_shape=jax.ShapeDtypeStruct((N_ATTN_HEADS, n_near, tk, tq), f32),
        compiler_params=_cparams(1),
        name="bias_tiles",
    )(rel_bias)


def _attn_kernel(tab_ref, lam_ref, q_ref, k_ref, vt_ref, bias_ref, w_ref, o_ref,
                 qcat_ref, s_ref, p_ref, m_ref, acc_ref, *, tq, tk, nk, out_scale):
    h = pl.program_id(1)
    qi = pl.program_id(2)
    r = tq // tk
    c_left = tab_ref[NUM_BUCKETS // 2 - 1, h]
    c_right = tab_ref[NUM_BUCKETS - 1, h]
    lam = lam_ref[0]

    qt = q_ref[0].astype(f32).T
    row = lax.broadcasted_iota(jnp.int32, qt.shape, 0)
    qcat_ref[...] = jnp.concatenate([jnp.where(row < ATTN_HEAD_DIM, qt, 0.0),
                                     jnp.where(row >= ATTN_HEAD_DIM, qt, 0.0)], axis=1).astype(bf16)
    ones = jnp.ones((8, tk), bf16)
    near_lo = jnp.maximum(qi * r - 1, 0)
    near_hi = jnp.minimum(qi * r + r + 1, nk)
    n_far = nk - (near_hi - near_lo)

    def logits(kt):
        return jnp.dot(k_ref[0, kt], qcat_ref[...], preferred_element_type=f32)

    def pv(kt, p):
        vext = jnp.concatenate([vt_ref[0, 0, kt], ones], axis=0)
        return jnp.dot(vext, p, preferred_element_type=f32)

    def tile_kt(t):
        far = t + jnp.where(t >= near_lo, near_hi - near_lo, 0)
        return jnp.where(t < n_far, far, near_lo + t - n_far)

    def tile_c(t):
        return jnp.where(t < n_far, jnp.where(t >= near_lo, c_right, c_left), 0.0)

    def stage_a(t, near):
        kt = tile_kt(t)
        s = logits(kt)
        if near:
            b = bias_ref[0, kt - qi * r + 1]
            s = s + jnp.concatenate([b, b], axis=1)
        s_ref[...] = s
        return jnp.max(s, axis=0, keepdims=True)

    def stage_b(t, smax):
        c = tile_c(t)
        m = m_ref[...]
        m_new = jnp.maximum(m, smax + c)
        p_ref[...] = jnp.exp(s_ref[...] - (m_new - c)).astype(bf16)
        m_ref[...] = m_new
        return jnp.exp(m - m_new)

    def stage_c(t, alpha):
        acc_ref[...] = acc_ref[...] * alpha + pv(tile_kt(t), p_ref[...])

    m_ref[...] = jnp.full(m_ref.shape, NEG, f32)
    acc_ref[...] = jnp.zeros(acc_ref.shape, f32)

    smax = stage_a(0, False)
    alpha = stage_b(0, smax)
    smax = stage_a(1, False)

    def body(near):
        def run(t, carry):
            alpha, smax = carry
            stage_c(t - 2, alpha)
            alpha = stage_b(t - 1, smax)
            smax = stage_a(t, near)
            return alpha, smax
        return run

    carry = lax.fori_loop(2, n_far, body(False), (alpha, smax))
    alpha, smax = lax.fori_loop(n_far, nk, body(True), carry)
    stage_c(nk - 2, alpha)
    alpha = stage_b(nk - 1, smax)
    stage_c(nk - 1, alpha)

    acc = acc_ref[...]
    o = acc[:ATTN_V_DIM] * (1.0 / acc[ATTN_V_DIM:ATTN_V_DIM + 1])
    diff = (o[:, :tq] - lam * o[:, tq:]).T
    o_ref[0] = (_rms(diff, w_ref[...]) * out_scale).astype(bf16)


def _attention(qk3, vt, bias, rel_bias, lam, subln_w, lambda_init, tq, tk):
    bsz, seq = qk3.shape[0], qk3.shape[1]
    nk = seq // tk
    assert nk >= tq // tk + 4, "the far-tile pipeline needs at least two far key tiles per query tile"
    qk4 = qk3.reshape(bsz, nk, tk, 2 * ATTN_WIDTH)
    kern = functools.partial(_attn_kernel, tq=tq, tk=tk, nk=nk, out_scale=1.0 - lambda_init)
    return pl.pallas_call(
        kern,
        grid=(bsz, N_ATTN_HEADS, seq // tq),
        in_specs=[
            pl.BlockSpec(memory_space=pltpu.SMEM),
            pl.BlockSpec(memory_space=pltpu.SMEM),
            pl.BlockSpec((1, tq, ATTN_V_DIM), lambda b, h, i: (b, i, h)),
            pl.BlockSpec((1, nk, tk, ATTN_V_DIM), lambda b, h, i: (b, 0, 0, N_ATTN_HEADS + h)),
            pl.BlockSpec((1, 1, nk, ATTN_V_DIM, tk), lambda b, h, i: (b, h, 0, 0, 0)),
            pl.BlockSpec((1, tq // tk + 2, tk, tq), lambda b, h, i: (h, 0, 0, 0)),
            pl.BlockSpec((1, ATTN_V_DIM), lambda b, h, i: (0, 0)),
        ],
        out_specs=pl.BlockSpec((1, tq, ATTN_V_DIM), lambda b, h, i: (b, i, h)),
        out_shape=jax.ShapeDtypeStruct((bsz, seq, ATTN_WIDTH), bf16),
        scratch_shapes=[
            pltpu.VMEM((ATTN_V_DIM, 2 * tq), bf16),
            pltpu.VMEM((tk, 2 * tq), f32),
            pltpu.VMEM((tk, 2 * tq), bf16),
            pltpu.VMEM((1, 2 * tq), f32),
            pltpu.VMEM((ATTN_V_DIM + 8, 2 * tq), f32),
        ],
        compiler_params=_cparams(3),
        name="attention",
    )(rel_bias, lam, qk3, qk4, vt, bias, subln_w)


def _softplus(v):
    return jnp.maximum(v, 0.0) + jnp.log1p(jnp.exp(-jnp.abs(v)))


def _ssd_decay_terms(dt_ref, dtb_ref, alog_ref, tril_ref):
    dtv = _softplus(dt_ref[0] + dtb_ref[...])
    lane = lax.broadcasted_iota(jnp.int32, (1, LANES), 1)
    a_row = jnp.where(lane < 2 * N_SSM_HEADS, -jnp.exp(alog_ref[...]), 0.0)
    a = dtv * a_row
    cs = _dot3_left(tril_ref[...], a)
    return dtv, a, cs


def _ssd_fwd_kernel(cur_ref, prev_ref, next_ref, dt_ref, cw_ref, cb_ref, dtb_ref, alog_ref, dskip_ref,
                    tril_ref, triu_ref, ef_ref, shift_ref, y_ref, act_ref, state_ref):
    c = pl.program_id(1)
    nc = pl.num_programs(1)

    @pl.when(c == 0)
    def _():
        state_ref[...] = jnp.zeros_like(state_ref)

    pmask = (c > 0).astype(f32)
    nmask = (c < nc - 1).astype(f32)
    ext = jnp.concatenate([prev_ref[0] * pmask.astype(bf16), cur_ref[0], next_ref[0] * nmask.astype(bf16)],
                          axis=0)
    taps = jnp.dot(shift_ref[...], ext, preferred_element_type=f32)
    mid = D_CONV // 2
    conv = cur_ref[0].astype(f32) * cw_ref[mid:mid + 1, :] + cb_ref[...]
    for i, j in enumerate(_SIDE_TAPS):
        conv = conv + taps[i * CHUNK:(i + 1) * CHUNK, :] * cw_ref[j:j + 1, :]
    act = conv * (1.0 / (1.0 + jnp.exp(-conv)))
    actb = act.astype(bf16)
    act_ref[0] = actb

    dtv, a, cs = _ssd_decay_terms(dt_ref, dtb_ref, alog_ref, tril_ref)
    a_t = a.T
    dt_t = dtv.T
    cs_t = _dot3(a_t, triu_ref[...])
    ecs = cs - a
    ecs_t = cs_t - a_t

    tot = cs[CHUNK - 1:CHUNK, :]
    scale_off = _expand_heads(jnp.exp(cs), ef_ref[...], 0)
    scale_in = _expand_heads(dtv * jnp.exp(tot - cs), ef_ref[...], 0)
    carry_dec = scale_off[CHUNK - 1:CHUNK, :]

    li = lax.broadcasted_iota(jnp.int32, (CHUNK, CHUNK), 0)
    si = lax.broadcasted_iota(jnp.int32, (CHUNK, CHUNK), 1)
    lower = li >= si
    upper = si >= li

    for g in range(N_SSM_GROUPS):
        gs = slice(g * GROUP_W, (g + 1) * GROUP_W)
        bg = actb[:, SSM_WIDTH + g * D_STATE: SSM_WIDTH + (g + 1) * D_STATE]
        cg = actb[:, SSM_WIDTH + N_SSM_GROUPS * D_STATE + g * D_STATE:
                  SSM_WIDTH + N_SSM_GROUPS * D_STATE + (g + 1) * D_STATE]
        cbm = lax.dot_general(cg, bg, (((1,), (1,)), ((), ())), preferred_element_type=f32)
        ws, xs_blocks = [], []
        for hh in range(SSM_HPG):
            hd = g * SSM_HPG + hh
            hb = N_SSM_HEADS + hd
            dec_f = jnp.exp(jnp.where(lower, cs[:, hd:hd + 1] - cs_t[hd:hd + 1, :], NEG))
            dec_b = jnp.exp(jnp.where(upper, ecs_t[hb:hb + 1, :] - ecs[:, hb:hb + 1], NEG))
            w = cbm * (dec_f * dt_t[hd:hd + 1, :] + dec_b * dt_t[hb:hb + 1, :])
            ws.append(w.astype(bf16))
            lane = lax.broadcasted_iota(jnp.int32, (CHUNK, GROUP_W), 1)
            own = (lane >= hh * SSM_HEAD_DIM) & (lane < (hh + 1) * SSM_HEAD_DIM)
            xs_blocks.append(jnp.where(own, actb[:, gs], jnp.zeros((), bf16)))
        xg = act[:, gs]
        wcat = jnp.concatenate(ws, axis=1)
        xblk = jnp.concatenate(xs_blocks, axis=0)
        y = jnp.dot(wcat, xblk, preferred_element_type=f32)
        y = y + jnp.dot(cg, state_ref[g].astype(bf16), preferred_element_type=f32) * scale_off[:, gs]
        y_ref[0, :, gs] = y + xg * dskip_ref[:, gs]
        xs = (xg * scale_in[:, gs]).astype(bf16)
        new = jnp.dot(bg.astype(f32).T.astype(bf16), xs, preferred_element_type=f32)
        state_ref[g] = state_ref[g] * carry_dec[:, gs] + new


def _ssd_bwd_kernel(act_ref, dt_ref, y_ref, z_ref, dtb_ref, alog_ref, nw_ref, tril_ref, eb_ref,
                    o_ref, state_ref):
    c = pl.program_id(1)

    @pl.when(c == 0)
    def _():
        state_ref[...] = jnp.zeros_like(state_ref)

    actb = act_ref[0]
    dtv, a, cs = _ssd_decay_terms(dt_ref, dtb_ref, alog_ref, tril_ref)
    ecs = cs - a
    tot = cs[CHUNK - 1:CHUNK, :]
    scale_off = _expand_heads(jnp.exp(tot - ecs), eb_ref[...], N_SSM_HEADS)
    scale_in = _expand_heads(dtv * jnp.exp(ecs), eb_ref[...], N_SSM_HEADS)
    carry_dec = scale_off[0:1, :]

    zf = z_ref[0].astype(f32)
    gate = zf * (1.0 / (1.0 + jnp.exp(-zf)))
    for g in range(N_SSM_GROUPS):
        gs = slice(g * GROUP_W, (g + 1) * GROUP_W)
        bg = actb[:, SSM_WIDTH + g * D_STATE: SSM_WIDTH + (g + 1) * D_STATE]
        cg = actb[:, SSM_WIDTH + N_SSM_GROUPS * D_STATE + g * D_STATE:
                  SSM_WIDTH + N_SSM_GROUPS * D_STATE + (g + 1) * D_STATE]
        y = y_ref[0, :, gs] + (jnp.dot(cg, state_ref[g].astype(bf16), preferred_element_type=f32)
                               * scale_off[:, gs])
        xs = (actb[:, gs].astype(f32) * scale_in[:, gs]).astype(bf16)
        new = jnp.dot(bg.astype(f32).T.astype(bf16), xs, preferred_element_type=f32)
        state_ref[g] = state_ref[g] * carry_dec[:, gs] + new
        o_ref[0, :, gs] = _rms(y * gate[:, gs], nw_ref[:, gs]).astype(bf16)


def _const_spec(shape):
    return pl.BlockSpec(shape, lambda b, c: (0,) * len(shape))


def _ssd_fwd(xz3, dt3, conv_w, conv_b, dt_bias, a_log, d_skip, consts):
    bsz, seq = xz3.shape[0], xz3.shape[1]
    nc = seq // CHUNK
    hb = CHUNK // HALO
    nh = seq // HALO
    tril, triu, ef, _, shift = consts
    const = _const_spec
    return pl.pallas_call(
        _ssd_fwd_kernel,
        grid=(bsz, nc),
        in_specs=[
            pl.BlockSpec((1, CHUNK, CONV_CH), lambda b, c: (b, c, 0)),
            pl.BlockSpec((1, HALO, CONV_CH), lambda b, c: (b, jnp.maximum(c * hb - 1, 0), 0)),
            pl.BlockSpec((1, HALO, CONV_CH), lambda b, c: (b, jnp.minimum((c + 1) * hb, nh - 1), 0)),
            pl.BlockSpec((1, CHUNK, LANES), lambda b, c: (b, c, 0)),
            const((8, CONV_CH)), const((1, CONV_CH)), const((1, LANES)), const((1, LANES)),
            const((1, SSM_WIDTH)), const((CHUNK, CHUNK)), const((CHUNK, CHUNK)), const((LANES, SSM_WIDTH)),
            const(((D_CONV - 1) * CHUNK, CHUNK + 2 * HALO)),
        ],
        out_specs=[
            pl.BlockSpec((1, CHUNK, SSM_WIDTH), lambda b, c: (b, c, 0)),
            pl.BlockSpec((1, CHUNK, CONV_CH), lambda b, c: (b, c, 0)),
        ],
        out_shape=[
            jax.ShapeDtypeStruct((bsz, seq, SSM_WIDTH), f32),
            jax.ShapeDtypeStruct((bsz, seq, CONV_CH), bf16),
        ],
        scratch_shapes=[pltpu.VMEM((N_SSM_GROUPS, D_STATE, GROUP_W), f32)],
        compiler_params=_cparams(2),
        name="ssd_fwd",
    )(xz3, xz3, xz3, dt3, conv_w, conv_b, dt_bias, a_log, d_skip, tril, triu, ef, shift)


def _ssd_bwd(act, dt3, y_part, xz3, dt_bias, a_log, norm_w, consts):
    bsz, seq = xz3.shape[0], xz3.shape[1]
    nc = seq // CHUNK
    tril, _, _, eb, _ = consts
    const = _const_spec
    rev = lambda b, c: (b, nc - 1 - c, 0)
    return pl.pallas_call(
        _ssd_bwd_kernel,
        grid=(bsz, nc),
        in_specs=[
            pl.BlockSpec((1, CHUNK, CONV_CH), rev),
            pl.BlockSpec((1, CHUNK, LANES), rev),
            pl.BlockSpec((1, CHUNK, SSM_WIDTH), rev),
            pl.BlockSpec((1, CHUNK, SSM_WIDTH), lambda b, c: (b, nc - 1 - c, CONV_CH // SSM_WIDTH)),
            const((1, LANES)), const((1, LANES)), const((1, SSM_WIDTH)),
            const((CHUNK, CHUNK)), const((LANES, SSM_WIDTH)),
        ],
        out_specs=pl.BlockSpec((1, CHUNK, SSM_WIDTH), rev),
        out_shape=jax.ShapeDtypeStruct((bsz, seq, SSM_WIDTH), bf16),
        scratch_shapes=[pltpu.VMEM((N_SSM_GROUPS, D_STATE, GROUP_W), f32)],
        compiler_params=_cparams(2),
        name="ssd_bwd",
    )(act, dt3, y_part, xz3, dt_bias, a_log, norm_w, tril, eb)


def _ssd(xz3, dt3, conv_w, conv_b, dt_bias, a_log, d_skip, norm_w, consts):
    y_part, act = _ssd_fwd(xz3, dt3, conv_w, conv_b, dt_bias, a_log, d_skip, consts)
    return _ssd_bwd(act, dt3, y_part, xz3, dt_bias, a_log, norm_w, consts)


def _ssd_consts():
    idx = np.arange(CHUNK)
    tril = (idx[None, :] <= idx[:, None]).astype(np.float32)
    head = np.arange(SSM_WIDTH) // SSM_HEAD_DIM
    rows = np.arange(LANES)
    piece = lambda first: (rows >= first) & (rows < first + 3 * N_SSM_HEADS)
    ef = (piece(0)[:, None] & (rows[:, None] % N_SSM_HEADS == head[None, :])).astype(np.float32)
    eb = (piece(N_SSM_HEADS)[:, None] & (rows[:, None] % N_SSM_HEADS == head[None, :])).astype(np.float32)
    out_row = np.arange((D_CONV - 1) * CHUNK)
    tap = np.asarray(_SIDE_TAPS)[out_row // CHUNK]
    src = HALO + out_row % CHUNK + tap - D_CONV // 2
    shift = (src[:, None] == np.arange(CHUNK + 2 * HALO)[None, :]).astype(np.float32)
    return tuple(jnp.asarray(m, bf16) for m in (tril, tril.T, ef, eb, shift))


def _out_proj_kernel(a_ref, s_ref, w_ref, x_ref, g_ref, o_ref):
    mix = jnp.dot(a_ref[...], w_ref[:ATTN_WIDTH, :], preferred_element_type=f32)
    mix = mix + jnp.dot(s_ref[...], w_ref[ATTN_WIDTH:, :], preferred_element_type=f32)
    o_ref[...] = x_ref[...] + _rms(mix, g_ref[...])


def _out_proj(a2, s2, w_out, x2, gain, tm):
    rows = x2.shape[0]
    return pl.pallas_call(
        _out_proj_kernel,
        grid=(rows // tm,),
        in_specs=[
            pl.BlockSpec((tm, ATTN_WIDTH), lambda i: (i, 0)),
            pl.BlockSpec((tm, SSM_WIDTH), lambda i: (i, 0)),
            pl.BlockSpec((D_MODEL, D_MODEL), lambda i: (0, 0)),
            pl.BlockSpec((tm, D_MODEL), lambda i: (i, 0)),
            pl.BlockSpec((1, D_MODEL), lambda i: (0, 0)),
        ],
        out_specs=pl.BlockSpec((tm, D_MODEL), lambda i: (i, 0)),
        out_shape=jax.ShapeDtypeStruct((rows, D_MODEL), f32),
        compiler_params=_cparams(1),
        name="out_proj",
    )(a2, s2, w_out, x2, gain)


def _mlp_kernel(x_ref, gpre_ref, wup_ref, wdn_ref, gpost_ref, o_ref, h_ref):
    j = pl.program_id(1)

    @pl.when(j == 0)
    def _():
        h_ref[...] = _rms(x_ref[...], gpre_ref[...]).astype(bf16)
        o_ref[...] = jnp.zeros_like(o_ref)

    u = jnp.dot(h_ref[...], wup_ref[...], preferred_element_type=f32)
    u = jnp.square(jnp.maximum(u, 0.0)).astype(bf16)
    o_ref[...] += jnp.dot(u, wdn_ref[...], preferred_element_type=f32)

    @pl.when(j == pl.num_programs(1) - 1)
    def _():
        o_ref[...] = x_ref[...] + _rms(o_ref[...], gpost_ref[...])


def _mlp(x2, g_pre, w_up, w_down, g_post, tm, tf):
    rows = x2.shape[0]
    return pl.pallas_call(
        _mlp_kernel,
        grid=(rows // tm, D_FF // tf),
        in_specs=[
            pl.BlockSpec((tm, D_MODEL), lambda i, j: (i, 0)),
            pl.BlockSpec((1, D_MODEL), lambda i, j: (0, 0)),
            pl.BlockSpec((D_MODEL, tf), lambda i, j: (0, j)),
            pl.BlockSpec((tf, D_MODEL), lambda i, j: (j, 0)),
            pl.BlockSpec((1, D_MODEL), lambda i, j: (0, 0)),
        ],
        out_specs=pl.BlockSpec((tm, D_MODEL), lambda i, j: (i, 0)),
        out_shape=jax.ShapeDtypeStruct((rows, D_MODEL), f32),
        scratch_shapes=[pltpu.VMEM((tm, D_MODEL), bf16)],
        compiler_params=_cparams(2),
        name="mlp",
    )(x2, g_pre, w_up, w_down, g_post)


def _attn_tiles(seq):
    for tq, tk in ((1024, 512), (512, 512), (256, 256)):
        if seq % tq == 0 and seq // tk >= tq // tk + 4:
            return tq, tk
    raise ValueError(f"sequence length {seq} is too short for the attention tiling")


def _trunk(x, layers, rel_bias, bias, consts, tq, tk):
    bsz, seq = x.shape[0], x.shape[1]
    rows = bsz * seq
    tm = tk
    tm_wide = 1024 if seq % 1024 == 0 and 1024 % tk == 0 else tm
    x2 = x.reshape(rows, D_MODEL)
    for lp in layers:
        qk, vt, xz, dt = _in_proj(x2, lp["pre_norm_mix"], lp["w_main"], lp["w_dt"], bsz, seq, tm_wide, tk)
        a_out = _attention(qk.reshape(bsz, seq, 2 * ATTN_WIDTH), vt, bias, rel_bias, lp["lam"], lp["attn_norm"],
                           lp["lambda_init"], tq, tk)
        s_out = _ssd(xz.reshape(bsz, seq, CONV_CH + SSM_WIDTH), dt.reshape(bsz, seq, LANES),
                     lp["conv_w"], lp["conv_b"], lp["dt_bias"], lp["a_log"], lp["d_skip"], lp["ssm_norm"],
                     consts)
        x2 = _out_proj(a_out.reshape(rows, ATTN_WIDTH), s_out.reshape(rows, SSM_WIDTH), lp["w_out"], x2,
                       lp["post_norm_mix"], tm)
        x2 = _mlp(x2, lp["pre_norm_mlp"], lp["w_up"], lp["w_down"], lp["post_norm_mlp"], tm_wide, 512)
    return x2.reshape(bsz, seq, D_MODEL)


def _pad_lanes(v, width=LANES):
    return jnp.pad(v, (0, width - v.shape[0]))[None, :].astype(f32)


def kernel(x_prompt, x_sample, rel_bias, pre_norm_mix, w_in, lambda_q1, lambda_k1, lambda_q2, lambda_k2,
           attn_norm, conv_w, conv_b, dt_bias_fwd, dt_bias_bwd, a_log_fwd, a_log_bwd, d_skip, ssm_norm,
           w_out, post_norm_mix, pre_norm_mlp, w_up, w_down, post_norm_mlp):
    depth = w_in.shape[0]
    row = lambda v: v[None, :].astype(f32)
    layers = []
    for i in range(depth):
        lambda_init = 0.8 - 0.6 * math.exp(-0.3 * i)
        lam = (jnp.exp(jnp.sum(lambda_q1[i].astype(f32) * lambda_k1[i].astype(f32)))
               - jnp.exp(jnp.sum(lambda_q2[i].astype(f32) * lambda_k2[i].astype(f32))) + lambda_init)
        wi = w_in[i]
        w_main = jnp.concatenate([wi[:, OFF_Q:OFF_Z], wi[:, OFF_XBC:OFF_DT], wi[:, OFF_Z:OFF_XBC]],
                                 axis=1).astype(bf16)
        w_dt = jnp.pad(wi[:, OFF_DT:], ((0, 0), (0, LANES - 2 * N_SSM_HEADS))).astype(bf16)
        layers.append(dict(
            lambda_init=lambda_init,
            lam=lam.reshape(1).astype(f32),
            pre_norm_mix=row(pre_norm_mix[i]), w_main=w_main, w_dt=w_dt,
            attn_norm=row(attn_norm[i]),
            conv_w=jnp.pad(conv_w[i].astype(f32), ((0, 8 - D_CONV), (0, 0))), conv_b=row(conv_b[i]),
            dt_bias=_pad_lanes(jnp.concatenate([dt_bias_fwd[i], dt_bias_bwd[i]])),
            a_log=_pad_lanes(jnp.concatenate([a_log_fwd[i], a_log_bwd[i]])),
            d_skip=row(jnp.repeat(d_skip[i], SSM_HEAD_DIM)), ssm_norm=row(ssm_norm[i]),
            w_out=w_out[i].astype(bf16), post_norm_mix=row(post_norm_mix[i]),
            pre_norm_mlp=row(pre_norm_mlp[i]), w_up=w_up[i].astype(bf16), w_down=w_down[i].astype(bf16),
            post_norm_mlp=row(post_norm_mlp[i]),
        ))
    consts = _ssd_consts()
    rel = rel_bias.astype(f32)
    outs = []
    biases = {}
    for x in (x_prompt, x_sample):
        tq, tk = _attn_tiles(x.shape[1])
        if (tq, tk) not in biases:
            biases[(tq, tk)] = _bias_tiles(rel, tq, tk)
        outs.append(_trunk(x, layers, rel, biases[(tq, tk)], consts, tq, tk))
    return tuple(outs)
```

```python
import functools
import math

import jax
import jax.numpy as jnp
import numpy as np
from jax import lax
from jax.experimental import pallas as pl
from jax.experimental.pallas import tpu as pltpu

f32 = jnp.float32
bf16 = jnp.bfloat16

D_MODEL = 2048
N_ATTN_HEADS = 8
ATTN_HEAD_DIM = 64
ATTN_V_DIM = 2 * ATTN_HEAD_DIM
ATTN_WIDTH = N_ATTN_HEADS * ATTN_V_DIM
SSM_WIDTH = 1024
SSM_HEAD_DIM = 64
N_SSM_HEADS = 16
N_SSM_GROUPS = 4
SSM_HPG = 4
D_STATE = 128
D_CONV = 5
CHUNK = 128
CONV_CH = SSM_WIDTH + 2 * N_SSM_GROUPS * D_STATE
GROUP_W = SSM_HPG * SSM_HEAD_DIM
OFF_Q, OFF_K, OFF_V, OFF_Z = 0, 1024, 2048, 3072
OFF_XBC = 4096
OFF_DT = OFF_XBC + CONV_CH
D_FF = 4 * D_MODEL
NUM_BUCKETS = 32
MAX_DISTANCE = 128
EPS = 1e-6
NEG = -1e30

LANES = 128
FAR_UNROLL = 2
HALO = 16
_SIDE_TAPS = tuple(j for j in range(D_CONV) if j != D_CONV // 2)
VMEM_LIMIT = 56 * 1024 * 1024


def _cparams(n_axes):
    return pltpu.CompilerParams(dimension_semantics=("arbitrary",) * n_axes,
                                vmem_limit_bytes=VMEM_LIMIT)


def _rms(xf, g):
    ms = jnp.mean(xf * xf, axis=-1, keepdims=True)
    return (xf * lax.rsqrt(ms + EPS)) * g


def _split3(v):
    hi = v.astype(bf16)
    r1 = v - hi.astype(f32)
    mid = r1.astype(bf16)
    lo = (r1 - mid.astype(f32)).astype(bf16)
    return hi, mid, lo


def _dot3(v, m01):
    hi, mid, lo = _split3(v)
    d = lambda a: jnp.dot(a, m01, preferred_element_type=f32)
    return d(hi) + d(mid) + d(lo)


def _expand_heads(v, e3, first):
    lane = lax.broadcasted_iota(jnp.int32, v.shape, 1)
    hi, mid, lo = _split3(jnp.where((lane >= first) & (lane < first + N_SSM_HEADS), v, 0.0))
    packed = (hi.astype(f32) + pltpu.roll(mid.astype(f32), N_SSM_HEADS, 1)
              + pltpu.roll(lo.astype(f32), 2 * N_SSM_HEADS, 1))
    return jnp.dot(packed.astype(bf16), e3, preferred_element_type=f32)


def _dot3_left(m01, v):
    hi, mid, lo = _split3(v)
    d = lambda a: jnp.dot(m01, a, preferred_element_type=f32)
    return d(hi) + d(mid) + d(lo)


def _in_proj_kernel(x_ref, g_ref, w_ref, wdt_ref, qk_ref, vt_ref, xz_ref, dt_ref, h_ref):
    j = pl.program_id(1)

    @pl.when(j == 0)
    def _():
        hb = _rms(x_ref[...], g_ref[...]).astype(bf16)
        h_ref[...] = hb
        dt_ref[...] = jnp.dot(hb, wdt_ref[...], preferred_element_type=f32)

    acc = jnp.dot(h_ref[...], w_ref[...], preferred_element_type=f32)

    @pl.when(j == 0)
    def _():
        qk_ref[...] = (acc * (ATTN_HEAD_DIM ** -0.5)).astype(bf16)

    @pl.when(j == 1)
    def _():
        qk_ref[...] = acc.astype(bf16)

    @pl.when(j == 2)
    def _():
        vt = acc.T.astype(bf16)
        tk = vt_ref.shape[-1]
        for u in range(vt_ref.shape[2]):
            vt_ref[0, :, u] = vt[:, u * tk:(u + 1) * tk].reshape(N_ATTN_HEADS, ATTN_V_DIM, tk)

    @pl.when(j >= 3)
    def _():
        xz_ref[...] = acc.astype(bf16)


def _in_proj(x2, gain, w_main, w_dt, bsz, seq, tm, tk):
    rows = x2.shape[0]
    nl = seq // tm
    nj = OFF_DT // ATTN_WIDTH
    n_qkv, z_tile = OFF_Z // ATTN_WIDTH, OFF_Z // ATTN_WIDTH
    w_tile = lambda i, j: (0, jnp.where(j < n_qkv, j, jnp.where(j < nj - 1, j + 1, z_tile)))
    return pl.pallas_call(
        _in_proj_kernel,
        grid=(rows // tm, nj),
        in_specs=[
            pl.BlockSpec((tm, D_MODEL), lambda i, j: (i, 0)),
            pl.BlockSpec((1, D_MODEL), lambda i, j: (0, 0)),
            pl.BlockSpec((D_MODEL, ATTN_WIDTH), w_tile),
            pl.BlockSpec((D_MODEL, LANES), lambda i, j: (0, 0)),
        ],
        out_specs=[
            pl.BlockSpec((tm, ATTN_WIDTH), lambda i, j: (i, jnp.minimum(j, 1))),
            pl.BlockSpec((1, N_ATTN_HEADS, tm // tk, ATTN_V_DIM, tk), lambda i, j: (i // nl, 0, i % nl, 0, 0)),
            pl.BlockSpec((tm, ATTN_WIDTH), lambda i, j: (i, jnp.clip(j - 3, 0, 2))),
            pl.BlockSpec((tm, LANES), lambda i, j: (i, 0)),
        ],
        out_shape=[
            jax.ShapeDtypeStruct((rows, 2 * ATTN_WIDTH), bf16),
            jax.ShapeDtypeStruct((bsz, N_ATTN_HEADS, seq // tk, ATTN_V_DIM, tk), bf16),
            jax.ShapeDtypeStruct((rows, CONV_CH + SSM_WIDTH), bf16),
            jax.ShapeDtypeStruct((rows, LANES), f32),
        ],
        scratch_shapes=[pltpu.VMEM((tm, D_MODEL), bf16)],
        compiler_params=_cparams(2),
        name="in_proj",
    )(x2, gain, w_main, w_dt)


def _bias_block(tab_ref, h, delta):
    kk = lax.broadcasted_iota(jnp.int32, (MAX_DISTANCE, MAX_DISTANCE), 0)
    qq = lax.broadcasted_iota(jnp.int32, (MAX_DISTANCE, MAX_DISTANCE), 1)
    rel = delta * MAX_DISTANCE + kk - qq
    nb = NUM_BUCKETS // 2
    max_exact = nb // 2
    ret = jnp.where(rel > 0, nb, 0)
    n = jnp.abs(rel)
    nf = jnp.maximum(n, 1).astype(f32)
    large = max_exact + (jnp.log(nf / max_exact) / math.log(MAX_DISTANCE / max_exact)
                         * (nb - max_exact)).astype(jnp.int32)
    large = jnp.minimum(large, nb - 1)
    bucket = ret + jnp.where(n < max_exact, n, large)
    out = jnp.zeros(rel.shape, f32)
    for b in range(NUM_BUCKETS):
        out = jnp.where(bucket == b, tab_ref[b, h], out)
    return out


def _bias_kernel(tab_ref, o_ref, *, tq, tk):
    h = pl.program_id(0)
    blk = MAX_DISTANCE
    side = {-2: jnp.full((blk, blk), tab_ref[NUM_BUCKETS // 2 - 1, h], f32),
            2: jnp.full((blk, blk), tab_ref[NUM_BUCKETS - 1, h], f32)}
    band = {delta: _bias_block(tab_ref, h, delta) for delta in (-1, 0, 1)}
    for d in range(tq // tk + 2):
        for a in range(tk // blk):
            for b in range(tq // blk):
                delta = (d - 1) * (tk // blk) + a - b
                val = band[delta] if abs(delta) <= 1 else side[2 if delta > 0 else -2]
                o_ref[0, d, a * blk:(a + 1) * blk, b * blk:(b + 1) * blk] = val


def _bias_tiles(rel_bias, tq, tk):
    n_near = tq // tk + 2
    return pl.pallas_call(
        functools.partial(_bias_kernel, tq=tq, tk=tk),
        grid=(N_ATTN_HEADS,),
        in_specs=[pl.BlockSpec(memory_space=pltpu.SMEM)],
        out_specs=pl.BlockSpec((1, n_near, tk, tq), lambda h: (h, 0, 0, 0)),
        out_shape=jax.ShapeDtypeStruct((N_ATTN_HEADS, n_near, tk, tq), f32),
        compiler_params=_cparams(1),
        name="bias_tiles",
    )(rel_bias)


def _attn_kernel(tab_ref, lam_ref, q_ref, k_ref, vt_ref, bias_ref, w_ref, o_ref,
                 qcat_ref, s_ref, p_ref, m_ref, acc_ref, *, tq, tk, nk, out_scale):
    h = pl.program_id(1)
    qi = pl.program_id(2)
    r = tq // tk
    c_left = tab_ref[NUM_BUCKETS // 2 - 1, h]
    c_right = tab_ref[NUM_BUCKETS - 1, h]
    lam = lam_ref[0]

    qt = q_ref[0].astype(f32).T
    row = lax.broadcasted_iota(jnp.int32, qt.shape, 0)
    qcat_ref[...] = jnp.concatenate([jnp.where(row < ATTN_HEAD_DIM, qt, 0.0),
                                     jnp.where(row >= ATTN_HEAD_DIM, qt, 0.0)], axis=1).astype(bf16)
    ones = jnp.ones((8, tk), bf16)
    near_lo = jnp.maximum(qi * r - 1, 0)
    near_hi = jnp.minimum(qi * r + r + 1, nk)
    n_far = nk - (near_hi - near_lo)

    def logits(kt):
        return jnp.dot(k_ref[0, kt], qcat_ref[...], preferred_element_type=f32)

    def pv(kt, p):
        vext = jnp.concatenate([vt_ref[0, 0, kt], ones], axis=0)
        return jnp.dot(vext, p, preferred_element_type=f32)

    def tile_kt(t):
        far = t + jnp.where(t >= near_lo, near_hi - near_lo, 0)
        return jnp.where(t < n_far, far, near_lo + t - n_far)

    def tile_c(t):
        return jnp.where(t < n_far, jnp.where(t >= near_lo, c_right, c_left), 0.0)

    def stage_a(t, near):
        kt = tile_kt(t)
        s = logits(kt)
        if near:
            b = bias_ref[0, kt - qi * r + 1]
            s = s + jnp.concatenate([b, b], axis=1)
        s_ref[...] = s
        return jnp.max(s, axis=0, keepdims=True)

    def stage_b(t, smax):
        c = tile_c(t)
        m = m_ref[...]
        m_new = jnp.maximum(m, smax + c)
        p_ref[...] = jnp.exp(s_ref[...] - (m_new - c)).astype(bf16)
        m_ref[...] = m_new
        return jnp.exp(m - m_new)

    def stage_c(t, alpha):
        acc_ref[...] = acc_ref[...] * alpha + pv(tile_kt(t), p_ref[...])

    m_ref[...] = jnp.full(m_ref.shape, NEG, f32)
    acc_ref[...] = jnp.zeros(acc_ref.shape, f32)

    smax = stage_a(0, False)
    alpha = stage_b(0, smax)
    smax = stage_a(1, False)

    def step(t, carry, near):
        alpha, smax = carry
        stage_c(t - 2, alpha)
        alpha = stage_b(t - 1, smax)
        return alpha, stage_a(t, near)

    first = 2 + jnp.bitwise_and(n_far - 2, FAR_UNROLL - 1)
    carry = lax.fori_loop(2, first, lambda t, cr: step(t, cr, False), (alpha, smax))

    def far_group(j, cr):
        t = first + FAR_UNROLL * j
        for u in range(FAR_UNROLL):
            cr = step(t + u, cr, False)
        return cr

    n_groups = lax.shift_right_logical(n_far - first, FAR_UNROLL.bit_length() - 1)
    carry = lax.fori_loop(0, n_groups, far_group, carry)
    alpha, smax = lax.fori_loop(n_far, nk, lambda t, cr: step(t, cr, True), carry)
    stage_c(nk - 2, alpha)
    alpha = stage_b(nk - 1, smax)
    stage_c(nk - 1, alpha)

    acc = acc_ref[...]
    o = acc[:ATTN_V_DIM] * (1.0 / acc[ATTN_V_DIM:ATTN_V_DIM + 1])
    diff = (o[:, :tq] - lam * o[:, tq:]).T
    o_ref[0] = (_rms(diff, w_ref[...]) * out_scale).astype(bf16)


def _attention(qk3, vt, bias, rel_bias, lam, subln_w, lambda_init, tq, tk):
    bsz, seq = qk3.shape[0], qk3.shape[1]
    nk = seq // tk
    assert nk >= tq // tk + 4, "the far-tile pipeline needs at least two far key tiles per query tile"
    qk4 = qk3.reshape(bsz, nk, tk, 2 * ATTN_WIDTH)
    kern = functools.partial(_attn_kernel, tq=tq, tk=tk, nk=nk, out_scale=1.0 - lambda_init)
    return pl.pallas_call(
        kern,
        grid=(bsz, N_ATTN_HEADS, seq // tq),
        in_specs=[
            pl.BlockSpec(memory_space=pltpu.SMEM),
            pl.BlockSpec(memory_space=pltpu.SMEM),
            pl.BlockSpec((1, tq, ATTN_V_DIM), lambda b, h, i: (b, i, h)),
            pl.BlockSpec((1, nk, tk, ATTN_V_DIM), lambda b, h, i: (b, 0, 0, N_ATTN_HEADS + h)),
            pl.BlockSpec((1, 1, nk, ATTN_V_DIM, tk), lambda b, h, i: (b, h, 0, 0, 0)),
            pl.BlockSpec((1, tq // tk + 2, tk, tq), lambda b, h, i: (h, 0, 0, 0)),
            pl.BlockSpec((1, ATTN_V_DIM), lambda b, h, i: (0, 0)),
        ],
        out_specs=pl.BlockSpec((1, tq, ATTN_V_DIM), lambda b, h, i: (b, i, h)),
        out_shape=jax.ShapeDtypeStruct((bsz, seq, ATTN_WIDTH), bf16),
        scratch_shapes=[
            pltpu.VMEM((ATTN_V_DIM, 2 * tq), bf16),
            pltpu.VMEM((tk, 2 * tq), f32),
            pltpu.VMEM((tk, 2 * tq), bf16),
            pltpu.VMEM((1, 2 * tq), f32),
            pltpu.VMEM((ATTN_V_DIM + 8, 2 * tq), f32),
        ],
        compiler_params=_cparams(3),
        name="attention",
    )(rel_bias, lam, qk3, qk4, vt, bias, subln_w)


def _softplus(v):
    return jnp.maximum(v, 0.0) + jnp.log1p(jnp.exp(-jnp.abs(v)))


def _ssd_decay_terms(dt_ref, dtb_ref, alog_ref, tril_ref):
    dtv = _softplus(dt_ref[0] + dtb_ref[...])
    lane = lax.broadcasted_iota(jnp.int32, (1, LANES), 1)
    a_row = jnp.where(lane < 2 * N_SSM_HEADS, -jnp.exp(alog_ref[...]), 0.0)
    a = dtv * a_row
    cs = _dot3_left(tril_ref[...], a)
    return dtv, a, cs


def _ssd_fwd_kernel(cur_ref, prev_ref, next_ref, dt_ref, cw_ref, cb_ref, dtb_ref, alog_ref, dskip_ref,
                    tril_ref, triu_ref, ef_ref, shift_ref, y_ref, act_ref, state_ref):
    c = pl.program_id(1)
    nc = pl.num_programs(1)

    @pl.when(c == 0)
    def _():
        state_ref[...] = jnp.zeros_like(state_ref)

    pmask = (c > 0).astype(f32)
    nmask = (c < nc - 1).astype(f32)
    ext = jnp.concatenate([prev_ref[0] * pmask.astype(bf16), cur_ref[0], next_ref[0] * nmask.astype(bf16)],
                          axis=0)
    taps = jnp.dot(shift_ref[...], ext, preferred_element_type=f32)
    mid = D_CONV // 2
    conv = cur_ref[0].astype(f32) * cw_ref[mid:mid + 1, :] + cb_ref[...]
    for i, j in enumerate(_SIDE_TAPS):
        conv = conv + taps[i * CHUNK:(i + 1) * CHUNK, :] * cw_ref[j:j + 1, :]
    act = conv * (1.0 / (1.0 + jnp.exp(-conv)))
    actb = act.astype(bf16)
    act_ref[0] = actb

    dtv, a, cs = _ssd_decay_terms(dt_ref, dtb_ref, alog_ref, tril_ref)
    a_t = a.T
    dt_t = dtv.T
    cs_t = _dot3(a_t, triu_ref[...])
    ecs = cs - a
    ecs_t = cs_t - a_t

    tot = cs[CHUNK - 1:CHUNK, :]
    scale_off = _expand_heads(jnp.exp(cs), ef_ref[...], 0)
    scale_in = _expand_heads(dtv * jnp.exp(tot - cs), ef_ref[...], 0)
    carry_dec = scale_off[CHUNK - 1:CHUNK, :]

    li = lax.broadcasted_iota(jnp.int32, (CHUNK, CHUNK), 0)
    si = lax.broadcasted_iota(jnp.int32, (CHUNK, CHUNK), 1)
    lower = li >= si
    upper = si >= li

    for g in range(N_SSM_GROUPS):
        gs = slice(g * GROUP_W, (g + 1) * GROUP_W)
        bg = actb[:, SSM_WIDTH + g * D_STATE: SSM_WIDTH + (g + 1) * D_STATE]
        cg = actb[:, SSM_WIDTH + N_SSM_GROUPS * D_STATE + g * D_STATE:
                  SSM_WIDTH + N_SSM_GROUPS * D_STATE + (g + 1) * D_STATE]
        cbm = lax.dot_general(cg, bg, (((1,), (1,)), ((), ())), preferred_element_type=f32)
        ws, xs_blocks = [], []
        for hh in range(SSM_HPG):
            hd = g * SSM_HPG + hh
            hb = N_SSM_HEADS + hd
            dec_f = jnp.exp(jnp.where(lower, cs[:, hd:hd + 1] - cs_t[hd:hd + 1, :], NEG))
            dec_b = jnp.exp(jnp.where(upper, ecs_t[hb:hb + 1, :] - ecs[:, hb:hb + 1], NEG))
            w = cbm * (dec_f * dt_t[hd:hd + 1, :] + dec_b * dt_t[hb:hb + 1, :])
            ws.append(w.astype(bf16))
            lane = lax.broadcasted_iota(jnp.int32, (CHUNK, GROUP_W), 1)
            own = (lane >= hh * SSM_HEAD_DIM) & (lane < (hh + 1) * SSM_HEAD_DIM)
            xs_blocks.append(jnp.where(own, actb[:, gs], jnp.zeros((), bf16)))
        xg = act[:, gs]
        wcat = jnp.concatenate(ws, axis=1)
        xblk = jnp.concatenate(xs_blocks, axis=0)
        y = jnp.dot(wcat, xblk, preferred_element_type=f32)
        y = y + jnp.dot(cg, state_ref[g].astype(bf16), preferred_element_type=f32) * scale_off[:, gs]
        y_ref[0, :, gs] = y + xg * dskip_ref[:, gs]
        xs = (xg * scale_in[:, gs]).astype(bf16)
        new = jnp.dot(bg.astype(f32).T.astype(bf16), xs, preferred_element_type=f32)
        state_ref[g] = state_ref[g] * carry_dec[:, gs] + new


def _ssd_bwd_kernel(act_ref, dt_ref, y_ref, z_ref, dtb_ref, alog_ref, nw_ref, tril_ref, eb_ref,
                    o_ref, state_ref):
    c = pl.program_id(1)

    @pl.when(c == 0)
    def _():
        state_ref[...] = jnp.zeros_like(state_ref)

    actb = act_ref[0]
    dtv, a, cs = _ssd_decay_terms(dt_ref, dtb_ref, alog_ref, tril_ref)
    ecs = cs - a
    tot = cs[CHUNK - 1:CHUNK, :]
    scale_off = _expand_heads(jnp.exp(tot - ecs), eb_ref[...], N_SSM_HEADS)
    scale_in = _expand_heads(dtv * jnp.exp(ecs), eb_ref[...], N_SSM_HEADS)
    carry_dec = scale_off[0:1, :]

    zf = z_ref[0].astype(f32)
    gate = zf * (1.0 / (1.0 + jnp.exp(-zf)))
    for g in range(N_SSM_GROUPS):
        gs = slice(g * GROUP_W, (g + 1) * GROUP_W)
        bg = actb[:, SSM_WIDTH + g * D_STATE: SSM_WIDTH + (g + 1) * D_STATE]
        cg = actb[:, SSM_WIDTH + N_SSM_GROUPS * D_STATE + g * D_STATE:
                  SSM_WIDTH + N_SSM_GROUPS * D_STATE + (g + 1) * D_STATE]
        y = y_ref[0, :, gs] + (jnp.dot(cg, state_ref[g].astype(bf16), preferred_element_type=f32)
                               * scale_off[:, gs])
        xs = (actb[:, gs].astype(f32) * scale_in[:, gs]).astype(bf16)
        new = jnp.dot(bg.astype(f32).T.astype(bf16), xs, preferred_element_type=f32)
        state_ref[g] = state_ref[g] * carry_dec[:, gs] + new
        o_ref[0, :, gs] = _rms(y * gate[:, gs], nw_ref[:, gs]).astype(bf16)


def _const_spec(shape):
    return pl.BlockSpec(shape, lambda b, c: (0,) * len(shape))


def _ssd_fwd(xz3, dt3, conv_w, conv_b, dt_bias, a_log, d_skip, consts):
    bsz, seq = xz3.shape[0], xz3.shape[1]
    nc = seq // CHUNK
    hb = CHUNK // HALO
    nh = seq // HALO
    tril, triu, ef, _, shift = consts
    const = _const_spec
    return pl.pallas_call(
        _ssd_fwd_kernel,
        grid=(bsz, nc),
        in_specs=[
            pl.BlockSpec((1, CHUNK, CONV_CH), lambda b, c: (b, c, 0)),
            pl.BlockSpec((1, HALO, CONV_CH), lambda b, c: (b, jnp.maximum(c * hb - 1, 0), 0)),
            pl.BlockSpec((1, HALO, CONV_CH), lambda b, c: (b, jnp.minimum((c + 1) * hb, nh - 1), 0)),
            pl.BlockSpec((1, CHUNK, LANES), lambda b, c: (b, c, 0)),
            const((8, CONV_CH)), const((1, CONV_CH)), const((1, LANES)), const((1, LANES)),
            const((1, SSM_WIDTH)), const((CHUNK, CHUNK)), const((CHUNK, CHUNK)), const((LANES, SSM_WIDTH)),
            const(((D_CONV - 1) * CHUNK, CHUNK + 2 * HALO)),
        ],
        out_specs=[
            pl.BlockSpec((1, CHUNK, SSM_WIDTH), lambda b, c: (b, c, 0)),
            pl.BlockSpec((1, CHUNK, CONV_CH), lambda b, c: (b, c, 0)),
        ],
        out_shape=[
            jax.ShapeDtypeStruct((bsz, seq, SSM_WIDTH), f32),
            jax.ShapeDtypeStruct((bsz, seq, CONV_CH), bf16),
        ],
        scratch_shapes=[pltpu.VMEM((N_SSM_GROUPS, D_STATE, GROUP_W), f32)],
        compiler_params=_cparams(2),
        name="ssd_fwd",
    )(xz3, xz3, xz3, dt3, conv_w, conv_b, dt_bias, a_log, d_skip, tril, triu, ef, shift)


def _ssd_bwd(act, dt3, y_part, xz3, dt_bias, a_log, norm_w, consts):
    bsz, seq = xz3.shape[0], xz3.shape[1]
    nc = seq // CHUNK
    tril, _, _, eb, _ = consts
    const = _const_spec
    rev = lambda b, c: (b, nc - 1 - c, 0)
    return pl.pallas_call(
        _ssd_bwd_kernel,
        grid=(bsz, nc),
        in_specs=[
            pl.BlockSpec((1, CHUNK, CONV_CH), rev),
            pl.BlockSpec((1, CHUNK, LANES), rev),
            pl.BlockSpec((1, CHUNK, SSM_WIDTH), rev),
            pl.BlockSpec((1, CHUNK, SSM_WIDTH), lambda b, c: (b, nc - 1 - c, CONV_CH // SSM_WIDTH)),
            const((1, LANES)), const((1, LANES)), const((1, SSM_WIDTH)),
            const((CHUNK, CHUNK)), const((LANES, SSM_WIDTH)),
        ],
        out_specs=pl.BlockSpec((1, CHUNK, SSM_WIDTH), rev),
        out_shape=jax.ShapeDtypeStruct((bsz, seq, SSM_WIDTH), bf16),
        scratch_shapes=[pltpu.VMEM((N_SSM_GROUPS, D_STATE, GROUP_W), f32)],
        compiler_params=_cparams(2),
        name="ssd_bwd",
    )(act, dt3, y_part, xz3, dt_bias, a_log, norm_w, tril, eb)


def _ssd(xz3, dt3, conv_w, conv_b, dt_bias, a_log, d_skip, norm_w, consts):
    y_part, act = _ssd_fwd(xz3, dt3, conv_w, conv_b, dt_bias, a_log, d_skip, consts)
    return _ssd_bwd(act, dt3, y_part, xz3, dt_bias, a_log, norm_w, consts)


def _ssd_consts():
    idx = np.arange(CHUNK)
    tril = (idx[None, :] <= idx[:, None]).astype(np.float32)
    head = np.arange(SSM_WIDTH) // SSM_HEAD_DIM
    rows = np.arange(LANES)
    piece = lambda first: (rows >= first) & (rows < first + 3 * N_SSM_HEADS)
    ef = (piece(0)[:, None] & (rows[:, None] % N_SSM_HEADS == head[None, :])).astype(np.float32)
    eb = (piece(N_SSM_HEADS)[:, None] & (rows[:, None] % N_SSM_HEADS == head[None, :])).astype(np.float32)
    out_row = np.arange((D_CONV - 1) * CHUNK)
    tap = np.asarray(_SIDE_TAPS)[out_row // CHUNK]
    src = HALO + out_row % CHUNK + tap - D_CONV // 2
    shift = (src[:, None] == np.arange(CHUNK + 2 * HALO)[None, :]).astype(np.float32)
    return tuple(jnp.asarray(m, bf16) for m in (tril, tril.T, ef, eb, shift))


def _out_proj_kernel(a_ref, s_ref, w_ref, x_ref, g_ref, o_ref):
    mix = jnp.dot(a_ref[...], w_ref[:ATTN_WIDTH, :], preferred_element_type=f32)
    mix = mix + jnp.dot(s_ref[...], w_ref[ATTN_WIDTH:, :], preferred_element_type=f32)
    o_ref[...] = x_ref[...] + _rms(mix, g_ref[...])


def _out_proj(a2, s2, w_out, x2, gain, tm):
    rows = x2.shape[0]
    return pl.pallas_call(
        _out_proj_kernel,
        grid=(rows // tm,),
        in_specs=[
            pl.BlockSpec((tm, ATTN_WIDTH), lambda i: (i, 0)),
            pl.BlockSpec((tm, SSM_WIDTH), lambda i: (i, 0)),
            pl.BlockSpec((D_MODEL, D_MODEL), lambda i: (0, 0)),
            pl.BlockSpec((tm, D_MODEL), lambda i: (i, 0)),
            pl.BlockSpec((1, D_MODEL), lambda i: (0, 0)),
        ],
        out_specs=pl.BlockSpec((tm, D_MODEL), lambda i: (i, 0)),
        out_shape=jax.ShapeDtypeStruct((rows, D_MODEL), f32),
        compiler_params=_cparams(1),
        name="out_proj",
    )(a2, s2, w_out, x2, gain)


def _mlp_kernel(x_ref, gpre_ref, wup_ref, wdn_ref, gpost_ref, o_ref, h_ref):
    j = pl.program_id(1)

    @pl.when(j == 0)
    def _():
        h_ref[...] = _rms(x_ref[...], gpre_ref[...]).astype(bf16)
        o_ref[...] = jnp.zeros_like(o_ref)

    u = jnp.dot(h_ref[...], wup_ref[...], preferred_element_type=f32)
    u = jnp.square(jnp.maximum(u, 0.0)).astype(bf16)
    o_ref[...] += jnp.dot(u, wdn_ref[...], preferred_element_type=f32)

    @pl.when(j == pl.num_programs(1) - 1)
    def _():
        o_ref[...] = x_ref[...] + _rms(o_ref[...], gpost_ref[...])


def _mlp(x2, g_pre, w_up, w_down, g_post, tm, tf):
    rows = x2.shape[0]
    return pl.pallas_call(
        _mlp_kernel,
        grid=(rows // tm, D_FF // tf),
        in_specs=[
            pl.BlockSpec((tm, D_MODEL), lambda i, j: (i, 0)),
            pl.BlockSpec((1, D_MODEL), lambda i, j: (0, 0)),
            pl.BlockSpec((D_MODEL, tf), lambda i, j: (0, j)),
            pl.BlockSpec((tf, D_MODEL), lambda i, j: (j, 0)),
            pl.BlockSpec((1, D_MODEL), lambda i, j: (0, 0)),
        ],
        out_specs=pl.BlockSpec((tm, D_MODEL), lambda i, j: (i, 0)),
        out_shape=jax.ShapeDtypeStruct((rows, D_MODEL), f32),
        scratch_shapes=[pltpu.VMEM((tm, D_MODEL), bf16)],
        compiler_params=_cparams(2),
        name="mlp",
    )(x2, g_pre, w_up, w_down, g_post)


def _attn_tiles(seq):
    for tq, tk in ((1024, 512), (512, 512), (256, 256)):
        if seq % tq == 0 and seq // tk >= tq // tk + 4:
            return tq, tk
    raise ValueError(f"sequence length {seq} is too short for the attention tiling")


def _trunk(x, layers, rel_bias, bias, consts, tq, tk):
    bsz, seq = x.shape[0], x.shape[1]
    rows = bsz * seq
    tm = tk
    tm_wide = 1024 if seq % 1024 == 0 and 1024 % tk == 0 else tm
    x2 = x.reshape(rows, D_MODEL)
    for lp in layers:
        qk, vt, xz, dt = _in_proj(x2, lp["pre_norm_mix"], lp["w_main"], lp["w_dt"], bsz, seq, tm_wide, tk)
        a_out = _attention(qk.reshape(bsz, seq, 2 * ATTN_WIDTH), vt, bias, rel_bias, lp["lam"], lp["attn_norm"],
                           lp["lambda_init"], tq, tk)
        s_out = _ssd(xz.reshape(bsz, seq, CONV_CH + SSM_WIDTH), dt.reshape(bsz, seq, LANES),
                     lp["conv_w"], lp["conv_b"], lp["dt_bias"], lp["a_log"], lp["d_skip"], lp["ssm_norm"],
                     consts)
        x2 = _out_proj(a_out.reshape(rows, ATTN_WIDTH), s_out.reshape(rows, SSM_WIDTH), lp["w_out"], x2,
                       lp["post_norm_mix"], tm)
        x2 = _mlp(x2, lp["pre_norm_mlp"], lp["w_up"], lp["w_down"], lp["post_norm_mlp"], tm_wide, 512)
    return x2.reshape(bsz, seq, D_MODEL)


def _pad_lanes(v, width=LANES):
    return jnp.pad(v, (0, width - v.shape[0]))[None, :].astype(f32)


def kernel(x_prompt, x_sample, rel_bias, pre_norm_mix, w_in, lambda_q1, lambda_k1, lambda_q2, lambda_k2,
           attn_norm, conv_w, conv_b, dt_bias_fwd, dt_bias_bwd, a_log_fwd, a_log_bwd, d_skip, ssm_norm,
           w_out, post_norm_mix, pre_norm_mlp, w_up, w_down, post_norm_mlp):
    depth = w_in.shape[0]
    row = lambda v: v[None, :].astype(f32)
    layers = []
    for i in range(depth):
        lambda_init = 0.8 - 0.6 * math.exp(-0.3 * i)
        lam = (jnp.exp(jnp.sum(lambda_q1[i].astype(f32) * lambda_k1[i].astype(f32)))
               - jnp.exp(jnp.sum(lambda_q2[i].astype(f32) * lambda_k2[i].astype(f32))) + lambda_init)
        wi = w_in[i]
        w_main = wi.astype(bf16)
        w_dt = jnp.pad(wi[:, OFF_DT:], ((0, 0), (0, LANES - 2 * N_SSM_HEADS))).astype(bf16)
        layers.append(dict(
            lambda_init=lambda_init,
            lam=lam.reshape(1).astype(f32),
            pre_norm_mix=row(pre_norm_mix[i]), w_main=w_main, w_dt=w_dt,
            attn_norm=row(attn_norm[i]),
            conv_w=jnp.pad(conv_w[i].astype(f32), ((0, 8 - D_CONV), (0, 0))), conv_b=row(conv_b[i]),
            dt_bias=_pad_lanes(jnp.concatenate([dt_bias_fwd[i], dt_bias_bwd[i]])),
            a_log=_pad_lanes(jnp.concatenate([a_log_fwd[i], a_log_bwd[i]])),
            d_skip=row(jnp.repeat(d_skip[i], SSM_HEAD_DIM)), ssm_norm=row(ssm_norm[i]),
            w_out=w_out[i].astype(bf16), post_norm_mix=row(post_norm_mix[i]),
            pre_norm_mlp=row(pre_norm_mlp[i]), w_up=w_up[i].astype(bf16), w_down=w_down[i].astype(bf16),
            post_norm_mlp=row(post_norm_mlp[i]),
        ))
    consts = _ssd_consts()
    rel = rel_bias.astype(f32)
    outs = []
    biases = {}
    for x in (x_prompt, x_sample):
        tq, tk = _attn_tiles(x.shape[1])
        if (tq, tk) not in biases:
            biases[(tq, tk)] = _bias_tiles(rel, tq, tk)
        outs.append(_trunk(x, layers, rel, biases[(tq, tk)], consts, tq, tk))
    return tuple(outs)
```

```python
import functools
import math

import jax
import jax.numpy as jnp
import numpy as np
from jax import lax
from jax.experimental import pallas as pl
from jax.experimental.pallas import tpu as pltpu

f32 = jnp.float32
bf16 = jnp.bfloat16

D_MODEL = 2048
N_ATTN_HEADS = 8
ATTN_HEAD_DIM = 64
ATTN_V_DIM = 2 * ATTN_HEAD_DIM
ATTN_WIDTH = N_ATTN_HEADS * ATTN_V_DIM
SSM_WIDTH = 1024
SSM_HEAD_DIM = 64
N_SSM_HEADS = 16
N_SSM_GROUPS = 4
SSM_HPG = 4
D_STATE = 128
D_CONV = 5
CHUNK = 128
CONV_CH = SSM_WIDTH + 2 * N_SSM_GROUPS * D_STATE
GROUP_W = SSM_HPG * SSM_HEAD_DIM
OFF_Q, OFF_K, OFF_V, OFF_Z = 0, 1024, 2048, 3072
OFF_XBC = 4096
OFF_DT = OFF_XBC + CONV_CH
D_FF = 4 * D_MODEL
NUM_BUCKETS = 32
MAX_DISTANCE = 128
EPS = 1e-6
NEG = -1e30
LOG2E = math.log2(math.e)

LANES = 128
TILE_UNROLL = 2
HALO = 16
_SIDE_TAPS = tuple(j for j in range(D_CONV) if j != D_CONV // 2)
VMEM_LIMIT = 56 * 1024 * 1024


def _cparams(n_axes):
    return pltpu.CompilerParams(dimension_semantics=("arbitrary",) * n_axes,
                                vmem_limit_bytes=VMEM_LIMIT)


def _rms(xf, g):
    ms = jnp.mean(xf * xf, axis=-1, keepdims=True)
    return (xf * lax.rsqrt(ms + EPS)) * g


def _split3(v):
    hi = v.astype(bf16)
    r1 = v - hi.astype(f32)
    mid = r1.astype(bf16)
    lo = (r1 - mid.astype(f32)).astype(bf16)
    return hi, mid, lo


def _dot3(v, m01):
    hi, mid, lo = _split3(v)
    d = lambda a: jnp.dot(a, m01, preferred_element_type=f32)
    return d(hi) + d(mid) + d(lo)


def _expand_heads(v, e3, first):
    lane = lax.broadcasted_iota(jnp.int32, v.shape, 1)
    hi, mid, lo = _split3(jnp.where((lane >= first) & (lane < first + N_SSM_HEADS), v, 0.0))
    packed = (hi.astype(f32) + pltpu.roll(mid.astype(f32), N_SSM_HEADS, 1)
              + pltpu.roll(lo.astype(f32), 2 * N_SSM_HEADS, 1))
    return jnp.dot(packed.astype(bf16), e3, preferred_element_type=f32)


def _dot3_left(m01, v):
    hi, mid, lo = _split3(v)
    d = lambda a: jnp.dot(m01, a, preferred_element_type=f32)
    return d(hi) + d(mid) + d(lo)


def _in_proj_kernel(x_ref, g_ref, w_ref, wdt_ref, qk_ref, vt_ref, xz_ref, dt_ref, h_ref):
    j = pl.program_id(1)

    @pl.when(j == 0)
    def _():
        hb = _rms(x_ref[...], g_ref[...]).astype(bf16)
        h_ref[...] = hb
        dt_ref[...] = jnp.dot(hb, wdt_ref[...], preferred_element_type=f32)

    acc = jnp.dot(h_ref[...], w_ref[...], preferred_element_type=f32)

    @pl.when(j == 0)
    def _():
        qk_ref[...] = (acc * (ATTN_HEAD_DIM ** -0.5 * LOG2E)).astype(bf16)

    @pl.when(j == 1)
    def _():
        qk_ref[...] = acc.astype(bf16)

    @pl.when(j == 2)
    def _():
        vt = acc.T.astype(bf16)
        tk = vt_ref.shape[-1]
        for u in range(vt_ref.shape[2]):
            vt_ref[0, :, u] = vt[:, u * tk:(u + 1) * tk].reshape(N_ATTN_HEADS, ATTN_V_DIM, tk)

    @pl.when(j >= 3)
    def _():
        xz_ref[...] = acc.astype(bf16)


def _in_proj(x2, gain, w_main, w_dt, bsz, seq, tm, tk):
    rows = x2.shape[0]
    nl = seq // tm
    nj = OFF_DT // ATTN_WIDTH
    n_qkv, z_tile = OFF_Z // ATTN_WIDTH, OFF_Z // ATTN_WIDTH
    w_tile = lambda i, j: (0, jnp.where(j < n_qkv, j, jnp.where(j < nj - 1, j + 1, z_tile)))
    return pl.pallas_call(
        _in_proj_kernel,
        grid=(rows // tm, nj),
        in_specs=[
            pl.BlockSpec((tm, D_MODEL), lambda i, j: (i, 0)),
            pl.BlockSpec((1, D_MODEL), lambda i, j: (0, 0)),
            pl.BlockSpec((D_MODEL, ATTN_WIDTH), w_tile),
            pl.BlockSpec((D_MODEL, LANES), lambda i, j: (0, 0)),
        ],
        out_specs=[
            pl.BlockSpec((tm, ATTN_WIDTH), lambda i, j: (i, jnp.minimum(j, 1))),
            pl.BlockSpec((1, N_ATTN_HEADS, tm // tk, ATTN_V_DIM, tk), lambda i, j: (i // nl, 0, i % nl, 0, 0)),
            pl.BlockSpec((tm, ATTN_WIDTH), lambda i, j: (i, jnp.clip(j - 3, 0, 2))),
            pl.BlockSpec((tm, LANES), lambda i, j: (i, 0)),
        ],
        out_shape=[
            jax.ShapeDtypeStruct((rows, 2 * ATTN_WIDTH), bf16),
            jax.ShapeDtypeStruct((bsz, N_ATTN_HEADS, seq // tk, ATTN_V_DIM, tk), bf16),
            jax.ShapeDtypeStruct((rows, CONV_CH + SSM_WIDTH), bf16),
            jax.ShapeDtypeStruct((rows, LANES), f32),
        ],
        scratch_shapes=[pltpu.VMEM((tm, D_MODEL), bf16)],
        compiler_params=_cparams(2),
        name="in_proj",
    )(x2, gain, w_main, w_dt)


def _bias_block(tab_ref, h, delta):
    kk = lax.broadcasted_iota(jnp.int32, (MAX_DISTANCE, MAX_DISTANCE), 0)
    qq = lax.broadcasted_iota(jnp.int32, (MAX_DISTANCE, MAX_DISTANCE), 1)
    rel = delta * MAX_DISTANCE + kk - qq
    nb = NUM_BUCKETS // 2
    max_exact = nb // 2
    ret = jnp.where(rel > 0, nb, 0)
    n = jnp.abs(rel)
    nf = jnp.maximum(n, 1).astype(f32)
    large = max_exact + (jnp.log(nf / max_exact) / math.log(MAX_DISTANCE / max_exact)
                         * (nb - max_exact)).astype(jnp.int32)
    large = jnp.minimum(large, nb - 1)
    bucket = ret + jnp.where(n < max_exact, n, large)
    out = jnp.zeros(rel.shape, f32)
    for b in range(NUM_BUCKETS):
        out = jnp.where(bucket == b, tab_ref[b, h] * LOG2E, out)
    return out


def _bias_kernel(tab_ref, o_ref, *, tq, tk):
    h = pl.program_id(0)
    blk = MAX_DISTANCE
    side = {-2: jnp.full((blk, blk), tab_ref[NUM_BUCKETS // 2 - 1, h] * LOG2E, f32),
            2: jnp.full((blk, blk), tab_ref[NUM_BUCKETS - 1, h] * LOG2E, f32)}
    band = {delta: _bias_block(tab_ref, h, delta) for delta in (-1, 0, 1)}
    for d in range(tq // tk + 2):
        for a in range(tk // blk):
            for b in range(tq // blk):
                delta = (d - 1) * (tk // blk) + a - b
                val = band[delta] if abs(delta) <= 1 else side[2 if delta > 0 else -2]
                o_ref[0, d, a * blk:(a + 1) * blk, b * blk:(b + 1) * blk] = val


def _bias_tiles(rel_bias, tq, tk):
    n_near = tq // tk + 2
    return pl.pallas_call(
        functools.partial(_bias_kernel, tq=tq, tk=tk),
        grid=(N_ATTN_HEADS,),
        in_specs=[pl.BlockSpec(memory_space=pltpu.SMEM)],
        out_specs=pl.BlockSpec((1, n_near, tk, tq), lambda h: (h, 0, 0, 0)),
        out_shape=jax.ShapeDtypeStruct((N_ATTN_HEADS, n_near, tk, tq), f32),
        compiler_params=_cparams(1),
        name="bias_tiles",
    )(rel_bias)


def _attn_kernel(tab_ref, lam_ref, q_ref, k_ref, vt_ref, bias_ref, w_ref, o_ref,
                 qcat_ref, s_ref, p_ref, m_ref, acc_ref, *, tq, tk, nk, out_scale):
    h = pl.program_id(1)
    qi = pl.program_id(2)
    r = tq // tk
    c_left = tab_ref[NUM_BUCKETS // 2 - 1, h] * LOG2E
    c_right = tab_ref[NUM_BUCKETS - 1, h] * LOG2E
    lam = lam_ref[0]

    qt = q_ref[0].astype(f32).T
    row = lax.broadcasted_iota(jnp.int32, qt.shape, 0)
    qcat_ref[...] = jnp.concatenate([jnp.where(row < ATTN_HEAD_DIM, qt, 0.0),
                                     jnp.where(row >= ATTN_HEAD_DIM, qt, 0.0)], axis=1).astype(bf16)
    ones = jnp.ones((8, tk), bf16)
    near_lo = jnp.maximum(qi * r - 1, 0)
    near_hi = jnp.minimum(qi * r + r + 1, nk)
    n_far = nk - (near_hi - near_lo)

    def logits(kt):
        return jnp.dot(k_ref[0, kt], qcat_ref[...], preferred_element_type=f32)

    def pv(kt, p):
        vext = jnp.concatenate([vt_ref[0, 0, kt], ones], axis=0)
        return jnp.dot(vext, p, preferred_element_type=f32)

    def tile_kt(t):
        far = t + jnp.where(t >= near_lo, near_hi - near_lo, 0)
        return jnp.where(t < n_far, far, near_lo + t - n_far)

    def tile_c(t):
        return jnp.where(t < n_far, jnp.where(t >= near_lo, c_right, c_left), 0.0)

    def stage_a(t, near):
        kt = tile_kt(t)
        s = logits(kt)
        if near:
            b = bias_ref[0, kt - qi * r + 1]
            s = s + jnp.concatenate([b, b], axis=1)
        s_ref[...] = s
        return jnp.max(s, axis=0, keepdims=True)

    def stage_b(t, smax):
        c = tile_c(t)
        m = m_ref[...]
        m_new = jnp.maximum(m, smax + c)
        p_ref[...] = jnp.exp2(s_ref[...] - (m_new - c)).astype(bf16)
        m_ref[...] = m_new
        return jnp.exp2(m - m_new)

    def stage_c(t, alpha):
        acc_ref[...] = acc_ref[...] * alpha + pv(tile_kt(t), p_ref[...])

    m_ref[...] = jnp.full(m_ref.shape, NEG, f32)
    acc_ref[...] = jnp.zeros(acc_ref.shape, f32)

    smax = stage_a(0, False)
    alpha = stage_b(0, smax)
    smax = stage_a(1, False)

    def step(t, carry, near):
        alpha, smax = carry
        stage_c(t - 2, alpha)
        alpha = stage_b(t - 1, smax)
        return alpha, stage_a(t, near)

    def run_steps(lo, hi, near, carry):
        first = lo + jnp.bitwise_and(hi - lo, TILE_UNROLL - 1)
        carry = lax.fori_loop(lo, first, lambda t, cr: step(t, cr, near), carry)

        def group(j, cr):
            for u in range(TILE_UNROLL):
                cr = step(first + TILE_UNROLL * j + u, cr, near)
            return cr

        n_groups = lax.shift_right_logical(hi - first, TILE_UNROLL.bit_length() - 1)
        return lax.fori_loop(0, n_groups, group, carry)

    carry = run_steps(2, n_far, False, (alpha, smax))
    alpha, smax = run_steps(n_far, nk, True, carry)
    stage_c(nk - 2, alpha)
    alpha = stage_b(nk - 1, smax)
    stage_c(nk - 1, alpha)

    acc = acc_ref[...]
    o = acc[:ATTN_V_DIM] * (1.0 / acc[ATTN_V_DIM:ATTN_V_DIM + 1])
    diff = (o[:, :tq] - lam * o[:, tq:]).T
    o_ref[0] = (_rms(diff, w_ref[...]) * out_scale).astype(bf16)


def _attention(qk3, vt, bias, rel_bias, lam, subln_w, lambda_init, tq, tk):
    bsz, seq = qk3.shape[0], qk3.shape[1]
    nk = seq // tk
    assert nk >= tq // tk + 4, "the far-tile pipeline needs at least two far key tiles per query tile"
    qk4 = qk3.reshape(bsz, nk, tk, 2 * ATTN_WIDTH)
    kern = functools.partial(_attn_kernel, tq=tq, tk=tk, nk=nk, out_scale=1.0 - lambda_init)
    return pl.pallas_call(
        kern,
        grid=(bsz, N_ATTN_HEADS, seq // tq),
        in_specs=[
            pl.BlockSpec(memory_space=pltpu.SMEM),
            pl.BlockSpec(memory_space=pltpu.SMEM),
            pl.BlockSpec((1, tq, ATTN_V_DIM), lambda b, h, i: (b, i, h)),
            pl.BlockSpec((1, nk, tk, ATTN_V_DIM), lambda b, h, i: (b, 0, 0, N_ATTN_HEADS + h)),
            pl.BlockSpec((1, 1, nk, ATTN_V_DIM, tk), lambda b, h, i: (b, h, 0, 0, 0)),
            pl.BlockSpec((1, tq // tk + 2, tk, tq), lambda b, h, i: (h, 0, 0, 0)),
            pl.BlockSpec((1, ATTN_V_DIM), lambda b, h, i: (0, 0)),
        ],
        out_specs=pl.BlockSpec((1, tq, ATTN_V_DIM), lambda b, h, i: (b, i, h)),
        out_shape=jax.ShapeDtypeStruct((bsz, seq, ATTN_WIDTH), bf16),
        scratch_shapes=[
            pltpu.VMEM((ATTN_V_DIM, 2 * tq), bf16),
            pltpu.VMEM((tk, 2 * tq), f32),
            pltpu.VMEM((tk, 2 * tq), bf16),
            pltpu.VMEM((1, 2 * tq), f32),
            pltpu.VMEM((ATTN_V_DIM + 8, 2 * tq), f32),
        ],
        compiler_params=_cparams(3),
        name="attention",
    )(rel_bias, lam, qk3, qk4, vt, bias, subln_w)


def _softplus(v):
    return jnp.maximum(v, 0.0) + jnp.log1p(jnp.exp(-jnp.abs(v)))


def _ssd_decay_terms(dt_ref, dtb_ref, alog_ref, tril_ref):
    dtv = _softplus(dt_ref[0] + dtb_ref[...])
    lane = lax.broadcasted_iota(jnp.int32, (1, LANES), 1)
    a_row = jnp.where(lane < 2 * N_SSM_HEADS, -jnp.exp(alog_ref[...]), 0.0)
    a = dtv * a_row
    cs = _dot3_left(tril_ref[...], a)
    return dtv, a, cs


def _ssd_fwd_kernel(cur_ref, prev_ref, next_ref, dt_ref, cw_ref, cb_ref, dtb_ref, alog_ref, dskip_ref,
                    tril_ref, triu_ref, ef_ref, shift_ref, y_ref, act_ref, state_ref):
    c = pl.program_id(1)
    nc = pl.num_programs(1)

    @pl.when(c == 0)
    def _():
        state_ref[...] = jnp.zeros_like(state_ref)

    pmask = (c > 0).astype(f32)
    nmask = (c < nc - 1).astype(f32)
    ext = jnp.concatenate([prev_ref[0] * pmask.astype(bf16), cur_ref[0], next_ref[0] * nmask.astype(bf16)],
                          axis=0)
    taps = jnp.dot(shift_ref[...], ext, preferred_element_type=f32)
    mid = D_CONV // 2
    conv = cur_ref[0].astype(f32) * cw_ref[mid:mid + 1, :] + cb_ref[...]
    for i, j in enumerate(_SIDE_TAPS):
        conv = conv + taps[i * CHUNK:(i + 1) * CHUNK, :] * cw_ref[j:j + 1, :]
    act = conv * (1.0 / (1.0 + jnp.exp(-conv)))
    actb = act.astype(bf16)
    act_ref[0] = actb

    dtv, a, cs = _ssd_decay_terms(dt_ref, dtb_ref, alog_ref, tril_ref)
    a_t = a.T
    dt_t = dtv.T
    cs_t = _dot3(a_t, triu_ref[...])
    ecs = cs - a
    ecs_t = cs_t - a_t

    tot = cs[CHUNK - 1:CHUNK, :]
    scale_off = _expand_heads(jnp.exp(cs), ef_ref[...], 0)
    scale_in = _expand_heads(dtv * jnp.exp(tot - cs), ef_ref[...], 0)
    carry_dec = scale_off[CHUNK - 1:CHUNK, :]

    li = lax.broadcasted_iota(jnp.int32, (CHUNK, CHUNK), 0)
    si = lax.broadcasted_iota(jnp.int32, (CHUNK, CHUNK), 1)
    lower = li >= si
    upper = si >= li

    for g in range(N_SSM_GROUPS):
        gs = slice(g * GROUP_W, (g + 1) * GROUP_W)
        bg = actb[:, SSM_WIDTH + g * D_STATE: SSM_WIDTH + (g + 1) * D_STATE]
        cg = actb[:, SSM_WIDTH + N_SSM_GROUPS * D_STATE + g * D_STATE:
                  SSM_WIDTH + N_SSM_GROUPS * D_STATE + (g + 1) * D_STATE]
        cbm = lax.dot_general(cg, bg, (((1,), (1,)), ((), ())), preferred_element_type=f32)
        ws, xs_blocks = [], []
        for hh in range(SSM_HPG):
            hd = g * SSM_HPG + hh
            hb = N_SSM_HEADS + hd
            dec_f = jnp.exp(jnp.where(lower, cs[:, hd:hd + 1] - cs_t[hd:hd + 1, :], NEG))
            dec_b = jnp.exp(jnp.where(upper, ecs_t[hb:hb + 1, :] - ecs[:, hb:hb + 1], NEG))
            w = cbm * (dec_f * dt_t[hd:hd + 1, :] + dec_b * dt_t[hb:hb + 1, :])
            ws.append(w.astype(bf16))
            lane = lax.broadcasted_iota(jnp.int32, (CHUNK, GROUP_W), 1)
            own = (lane >= hh * SSM_HEAD_DIM) & (lane < (hh + 1) * SSM_HEAD_DIM)
            xs_blocks.append(jnp.where(own, actb[:, gs], jnp.zeros((), bf16)))
        xg = act[:, gs]
        wcat = jnp.concatenate(ws, axis=1)
        xblk = jnp.concatenate(xs_blocks, axis=0)
        y = jnp.dot(wcat, xblk, preferred_element_type=f32)
        y = y + jnp.dot(cg, state_ref[g].astype(bf16), preferred_element_type=f32) * scale_off[:, gs]
        y_ref[0, :, gs] = y + xg * dskip_ref[:, gs]
        xs = (xg * scale_in[:, gs]).astype(bf16)
        new = jnp.dot(bg.astype(f32).T.astype(bf16), xs, preferred_element_type=f32)
        state_ref[g] = state_ref[g] * carry_dec[:, gs] + new


def _ssd_bwd_kernel(act_ref, dt_ref, y_ref, z_ref, dtb_ref, alog_ref, nw_ref, tril_ref, eb_ref,
                    o_ref, state_ref):
    c = pl.program_id(1)

    @pl.when(c == 0)
    def _():
        state_ref[...] = jnp.zeros_like(state_ref)

    actb = act_ref[0]
    dtv, a, cs = _ssd_decay_terms(dt_ref, dtb_ref, alog_ref, tril_ref)
    ecs = cs - a
    tot = cs[CHUNK - 1:CHUNK, :]
    scale_off = _expand_heads(jnp.exp(tot - ecs), eb_ref[...], N_SSM_HEADS)
    scale_in = _expand_heads(dtv * jnp.exp(ecs), eb_ref[...], N_SSM_HEADS)
    carry_dec = scale_off[0:1, :]

    zf = z_ref[0].astype(f32)
    gate = zf * (1.0 / (1.0 + jnp.exp(-zf)))
    for g in range(N_SSM_GROUPS):
        gs = slice(g * GROUP_W, (g + 1) * GROUP_W)
        bg = actb[:, SSM_WIDTH + g * D_STATE: SSM_WIDTH + (g + 1) * D_STATE]
        cg = actb[:, SSM_WIDTH + N_SSM_GROUPS * D_STATE + g * D_STATE:
                  SSM_WIDTH + N_SSM_GROUPS * D_STATE + (g + 1) * D_STATE]
        y = y_ref[0, :, gs] + (jnp.dot(cg, state_ref[g].astype(bf16), preferred_element_type=f32)
                               * scale_off[:, gs])
        xs = (actb[:, gs].astype(f32) * scale_in[:, gs]).astype(bf16)
        new = jnp.dot(bg.astype(f32).T.astype(bf16), xs, preferred_element_type=f32)
        state_ref[g] = state_ref[g] * carry_dec[:, gs] + new
        o_ref[0, :, gs] = _rms(y * gate[:, gs], nw_ref[:, gs]).astype(bf16)


def _const_spec(shape):
    return pl.BlockSpec(shape, lambda b, c: (0,) * len(shape))


def _ssd_fwd(xz3, dt3, conv_w, conv_b, dt_bias, a_log, d_skip, consts):
    bsz, seq = xz3.shape[0], xz3.shape[1]
    nc = seq // CHUNK
    hb = CHUNK // HALO
    nh = seq // HALO
    tril, triu, ef, _, shift = consts
    const = _const_spec
    return pl.pallas_call(
        _ssd_fwd_kernel,
        grid=(bsz, nc),
        in_specs=[
            pl.BlockSpec((1, CHUNK, CONV_CH), lambda b, c: (b, c, 0)),
            pl.BlockSpec((1, HALO, CONV_CH), lambda b, c: (b, jnp.maximum(c * hb - 1, 0), 0)),
            pl.BlockSpec((1, HALO, CONV_CH), lambda b, c: (b, jnp.minimum((c + 1) * hb, nh - 1), 0)),
            pl.BlockSpec((1, CHUNK, LANES), lambda b, c: (b, c, 0)),
            const((8, CONV_CH)), const((1, CONV_CH)), const((1, LANES)), const((1, LANES)),
            const((1, SSM_WIDTH)), const((CHUNK, CHUNK)), const((CHUNK, CHUNK)), const((LANES, SSM_WIDTH)),
            const(((D_CONV - 1) * CHUNK, CHUNK + 2 * HALO)),
        ],
        out_specs=[
            pl.BlockSpec((1, CHUNK, SSM_WIDTH), lambda b, c: (b, c, 0)),
            pl.BlockSpec((1, CHUNK, CONV_CH), lambda b, c: (b, c, 0)),
        ],
        out_shape=[
            jax.ShapeDtypeStruct((bsz, seq, SSM_WIDTH), f32),
            jax.ShapeDtypeStruct((bsz, seq, CONV_CH), bf16),
        ],
        scratch_shapes=[pltpu.VMEM((N_SSM_GROUPS, D_STATE, GROUP_W), f32)],
        compiler_params=_cparams(2),
        name="ssd_fwd",
    )(xz3, xz3, xz3, dt3, conv_w, conv_b, dt_bias, a_log, d_skip, tril, triu, ef, shift)


def _ssd_bwd(act, dt3, y_part, xz3, dt_bias, a_log, norm_w, consts):
    bsz, seq = xz3.shape[0], xz3.shape[1]
    nc = seq // CHUNK
    tril, _, _, eb, _ = consts
    const = _const_spec
    rev = lambda b, c: (b, nc - 1 - c, 0)
    return pl.pallas_call(
        _ssd_bwd_kernel,
        grid=(bsz, nc),
        in_specs=[
            pl.BlockSpec((1, CHUNK, CONV_CH), rev),
            pl.BlockSpec((1, CHUNK, LANES), rev),
            pl.BlockSpec((1, CHUNK, SSM_WIDTH), rev),
            pl.BlockSpec((1, CHUNK, SSM_WIDTH), lambda b, c: (b, nc - 1 - c, CONV_CH // SSM_WIDTH)),
            const((1, LANES)), const((1, LANES)), const((1, SSM_WIDTH)),
            const((CHUNK, CHUNK)), const((LANES, SSM_WIDTH)),
        ],
        out_specs=pl.BlockSpec((1, CHUNK, SSM_WIDTH), rev),
        out_shape=jax.ShapeDtypeStruct((bsz, seq, SSM_WIDTH), bf16),
        scratch_shapes=[pltpu.VMEM((N_SSM_GROUPS, D_STATE, GROUP_W), f32)],
        compiler_params=_cparams(2),
        name="ssd_bwd",
    )(act, dt3, y_part, xz3, dt_bias, a_log, norm_w, tril, eb)


def _ssd(xz3, dt3, conv_w, conv_b, dt_bias, a_log, d_skip, norm_w, consts):
    y_part, act = _ssd_fwd(xz3, dt3, conv_w, conv_b, dt_bias, a_log, d_skip, consts)
    return _ssd_bwd(act, dt3, y_part, xz3, dt_bias, a_log, norm_w, consts)


def _ssd_consts():
    idx = np.arange(CHUNK)
    tril = (idx[None, :] <= idx[:, None]).astype(np.float32)
    head = np.arange(SSM_WIDTH) // SSM_HEAD_DIM
    rows = np.arange(LANES)
    piece = lambda first: (rows >= first) & (rows < first + 3 * N_SSM_HEADS)
    ef = (piece(0)[:, None] & (rows[:, None] % N_SSM_HEADS == head[None, :])).astype(np.float32)
    eb = (piece(N_SSM_HEADS)[:, None] & (rows[:, None] % N_SSM_HEADS == head[None, :])).astype(np.float32)
    out_row = np.arange((D_CONV - 1) * CHUNK)
    tap = np.asarray(_SIDE_TAPS)[out_row // CHUNK]
    src = HALO + out_row % CHUNK + tap - D_CONV // 2
    shift = (src[:, None] == np.arange(CHUNK + 2 * HALO)[None, :]).astype(np.float32)
    return tuple(jnp.asarray(m, bf16) for m in (tril, tril.T, ef, eb, shift))


def _out_proj_kernel(a_ref, s_ref, w_ref, x_ref, g_ref, o_ref):
    mix = jnp.dot(a_ref[...], w_ref[:ATTN_WIDTH, :], preferred_element_type=f32)
    mix = mix + jnp.dot(s_ref[...], w_ref[ATTN_WIDTH:, :], preferred_element_type=f32)
    o_ref[...] = x_ref[...] + _rms(mix, g_ref[...])


def _out_proj(a2, s2, w_out, x2, gain, tm):
    rows = x2.shape[0]
    return pl.pallas_call(
        _out_proj_kernel,
        grid=(rows // tm,),
        in_specs=[
            pl.BlockSpec((tm, ATTN_WIDTH), lambda i: (i, 0)),
            pl.BlockSpec((tm, SSM_WIDTH), lambda i: (i, 0)),
            pl.BlockSpec((D_MODEL, D_MODEL), lambda i: (0, 0)),
            pl.BlockSpec((tm, D_MODEL), lambda i: (i, 0)),
            pl.BlockSpec((1, D_MODEL), lambda i: (0, 0)),
        ],
        out_specs=pl.BlockSpec((tm, D_MODEL), lambda i: (i, 0)),
        out_shape=jax.ShapeDtypeStruct((rows, D_MODEL), f32),
        compiler_params=_cparams(1),
        name="out_proj",
    )(a2, s2, w_out, x2, gain)


def _mlp_kernel(x_ref, gpre_ref, wup_ref, wdn_ref, gpost_ref, o_ref, h_ref):
    j = pl.program_id(1)

    @pl.when(j == 0)
    def _():
        h_ref[...] = _rms(x_ref[...], gpre_ref[...]).astype(bf16)
        o_ref[...] = jnp.zeros_like(o_ref)

    u = jnp.dot(h_ref[...], wup_ref[...], preferred_element_type=f32)
    u = jnp.square(jnp.maximum(u, 0.0)).astype(bf16)
    o_ref[...] += jnp.dot(u, wdn_ref[...], preferred_element_type=f32)

    @pl.when(j == pl.num_programs(1) - 1)
    def _():
        o_ref[...] = x_ref[...] + _rms(o_ref[...], gpost_ref[...])


def _mlp(x2, g_pre, w_up, w_down, g_post, tm, tf):
    rows = x2.shape[0]
    return pl.pallas_call(
        _mlp_kernel,
        grid=(rows // tm, D_FF // tf),
        in_specs=[
            pl.BlockSpec((tm, D_MODEL), lambda i, j: (i, 0)),
            pl.BlockSpec((1, D_MODEL), lambda i, j: (0, 0)),
            pl.BlockSpec((D_MODEL, tf), lambda i, j: (0, j)),
            pl.BlockSpec((tf, D_MODEL), lambda i, j: (j, 0)),
            pl.BlockSpec((1, D_MODEL), lambda i, j: (0, 0)),
        ],
        out_specs=pl.BlockSpec((tm, D_MODEL), lambda i, j: (i, 0)),
        out_shape=jax.ShapeDtypeStruct((rows, D_MODEL), f32),
        scratch_shapes=[pltpu.VMEM((tm, D_MODEL), bf16)],
        compiler_params=_cparams(2),
        name="mlp",
    )(x2, g_pre, w_up, w_down, g_post)


def _attn_tiles(seq):
    for tq, tk in ((1024, 512), (512, 512), (256, 256)):
        if seq % tq == 0 and seq // tk >= tq // tk + 4:
            return tq, tk
    raise ValueError(f"sequence length {seq} is too short for the attention tiling")


def _trunk(x, layers, rel_bias, bias, consts, tq, tk):
    bsz, seq = x.shape[0], x.shape[1]
    rows = bsz * seq
    tm = tk
    tm_wide = 1024 if seq % 1024 == 0 and 1024 % tk == 0 else tm
    x2 = x.reshape(rows, D_MODEL)
    for lp in layers:
        qk, vt, xz, dt = _in_proj(x2, lp["pre_norm_mix"], lp["w_main"], lp["w_dt"], bsz, seq, tm_wide, tk)
        a_out = _attention(qk.reshape(bsz, seq, 2 * ATTN_WIDTH), vt, bias, rel_bias, lp["lam"], lp["attn_norm"],
                           lp["lambda_init"], tq, tk)
        s_out = _ssd(xz.reshape(bsz, seq, CONV_CH + SSM_WIDTH), dt.reshape(bsz, seq, LANES),
                     lp["conv_w"], lp["conv_b"], lp["dt_bias"], lp["a_log"], lp["d_skip"], lp["ssm_norm"],
                     consts)
        x2 = _out_proj(a_out.reshape(rows, ATTN_WIDTH), s_out.reshape(rows, SSM_WIDTH), lp["w_out"], x2,
                       lp["post_norm_mix"], tm)
        x2 = _mlp(x2, lp["pre_norm_mlp"], lp["w_up"], lp["w_down"], lp["post_norm_mlp"], tm_wide, 512)
    return x2.reshape(bsz, seq, D_MODEL)


def _pad_lanes(v, width=LANES):
    return jnp.pad(v, (0, width - v.shape[0]))[None, :].astype(f32)


def kernel(x_prompt, x_sample, rel_bias, pre_norm_mix, w_in, lambda_q1, lambda_k1, lambda_q2, lambda_k2,
           attn_norm, conv_w, conv_b, dt_bias_fwd, dt_bias_bwd, a_log_fwd, a_log_bwd, d_skip, ssm_norm,
           w_out, post_norm_mix, pre_norm_mlp, w_up, w_down, post_norm_mlp):
    depth = w_in.shape[0]
    row = lambda v: v[None, :].astype(f32)
    layers = []
    for i in range(depth):
        lambda_init = 0.8 - 0.6 * math.exp(-0.3 * i)
        lam = (jnp.exp(jnp.sum(lambda_q1[i].astype(f32) * lambda_k1[i].astype(f32)))
               - jnp.exp(jnp.sum(lambda_q2[i].astype(f32) * lambda_k2[i].astype(f32))) + lambda_init)
        wi = w_in[i]
        w_main = wi.astype(bf16)
        w_dt = jnp.pad(wi[:, OFF_DT:], ((0, 0), (0, LANES - 2 * N_SSM_HEADS))).astype(bf16)
        layers.append(dict(
            lambda_init=lambda_init,
            lam=lam.reshape(1).astype(f32),
            pre_norm_mix=row(pre_norm_mix[i]), w_main=w_main, w_dt=w_dt,
            attn_norm=row(attn_norm[i]),
            conv_w=jnp.pad(conv_w[i].astype(f32), ((0, 8 - D_CONV), (0, 0))), conv_b=row(conv_b[i]),
            dt_bias=_pad_lanes(jnp.concatenate([dt_bias_fwd[i], dt_bias_bwd[i]])),
            a_log=_pad_lanes(jnp.concatenate([a_log_fwd[i], a_log_bwd[i]])),
            d_skip=row(jnp.repeat(d_skip[i], SSM_HEAD_DIM)), ssm_norm=row(ssm_norm[i]),
            w_out=w_out[i].astype(bf16), post_norm_mix=row(post_norm_mix[i]),
            pre_norm_mlp=row(pre_norm_mlp[i]), w_up=w_up[i].astype(bf16), w_down=w_down[i].astype(bf16),
            post_norm_mlp=row(post_norm_mlp[i]),
        ))
    consts = _ssd_consts()
    rel = rel_bias.astype(f32)
    outs = []
    biases = {}
    for x in (x_prompt, x_sample):
        tq, tk = _attn_tiles(x.shape[1])
        if (tq, tk) not in biases:
            biases[(tq, tk)] = _bias_tiles(rel, tq, tk)
        outs.append(_trunk(x, layers, rel, biases[(tq, tk)], consts, tq, tk))
    return tuple(outs)
```

```python
import functools
import math

import jax
import jax.numpy as jnp
import numpy as np
from jax import lax
from jax.experimental import pallas as pl
from jax.experimental.pallas import tpu as pltpu

f32 = jnp.float32
bf16 = jnp.bfloat16

D_MODEL = 2048
N_ATTN_HEADS = 8
ATTN_HEAD_DIM = 64
ATTN_V_DIM = 2 * ATTN_HEAD_DIM
ATTN_WIDTH = N_ATTN_HEADS * ATTN_V_DIM
SSM_WIDTH = 1024
SSM_HEAD_DIM = 64
N_SSM_HEADS = 16
N_SSM_GROUPS = 4
SSM_HPG = 4
D_STATE = 128
D_CONV = 5
CHUNK = 128
CONV_CH = SSM_WIDTH + 2 * N_SSM_GROUPS * D_STATE
GROUP_W = SSM_HPG * SSM_HEAD_DIM
OFF_Q, OFF_K, OFF_V, OFF_Z = 0, 1024, 2048, 3072
OFF_XBC = 4096
OFF_DT = OFF_XBC + CONV_CH
D_FF = 4 * D_MODEL
NUM_BUCKETS = 32
MAX_DISTANCE = 128
EPS = 1e-6
NEG = -1e30
LOG2E = math.log2(math.e)

LANES = 128
SUBLANES = 8
ROW_TILE = 1024
FF_TILE = 512
TILE_UNROLL = 2
HALO = 16
_SIDE_TAPS = tuple(j for j in range(D_CONV) if j != D_CONV // 2)
VMEM_LIMIT = 56 * 1024 * 1024


def _cparams(n_axes):
    return pltpu.CompilerParams(dimension_semantics=("arbitrary",) * n_axes,
                                vmem_limit_bytes=VMEM_LIMIT)


def _rms(xf, g):
    ms = jnp.mean(xf * xf, axis=-1, keepdims=True)
    return (xf * lax.rsqrt(ms + EPS)) * g


def _split3(v):
    hi = v.astype(bf16)
    r1 = v - hi.astype(f32)
    mid = r1.astype(bf16)
    lo = (r1 - mid.astype(f32)).astype(bf16)
    return hi, mid, lo


def _dot3(v, m01):
    hi, mid, lo = _split3(v)
    d = lambda a: jnp.dot(a, m01, preferred_element_type=f32)
    return d(hi) + d(mid) + d(lo)


def _expand_heads(v, e3, first):
    lane = lax.broadcasted_iota(jnp.int32, v.shape, 1)
    hi, mid, lo = _split3(jnp.where((lane >= first) & (lane < first + N_SSM_HEADS), v, 0.0))
    packed = (hi.astype(f32) + pltpu.roll(mid.astype(f32), N_SSM_HEADS, 1)
              + pltpu.roll(lo.astype(f32), 2 * N_SSM_HEADS, 1))
    return jnp.dot(packed.astype(bf16), e3, preferred_element_type=f32)


def _dot3_left(m01, v):
    hi, mid, lo = _split3(v)
    d = lambda a: jnp.dot(m01, a, preferred_element_type=f32)
    return d(hi) + d(mid) + d(lo)


def _in_proj_kernel(x_ref, g_ref, w_ref, wdt_ref, qk_ref, vt_ref, xz_ref, dt_ref, h_ref):
    j = pl.program_id(1)
    half = x_ref.shape[0] // 2

    @pl.when(j == 0)
    def _():
        for rows in (slice(0, half), slice(half, 2 * half)):
            hb = _rms(x_ref[rows], g_ref[...]).astype(bf16)
            h_ref[rows] = hb
            dt_ref[rows] = jnp.dot(hb, wdt_ref[...], preferred_element_type=f32)
            q = jnp.dot(hb, w_ref[...], preferred_element_type=f32)
            qk_ref[rows] = (q * (ATTN_HEAD_DIM ** -0.5 * LOG2E)).astype(bf16)

    @pl.when(j > 0)
    def _():
        acc = jnp.dot(h_ref[...], w_ref[...], preferred_element_type=f32)

        @pl.when(j == 1)
        def _():
            qk_ref[...] = acc.astype(bf16)

        @pl.when(j == 2)
        def _():
            vt = acc.T.astype(bf16)
            tk = vt_ref.shape[-1]
            for u in range(vt_ref.shape[2]):
                vt_ref[0, :, u] = vt[:, u * tk:(u + 1) * tk].reshape(N_ATTN_HEADS, ATTN_V_DIM, tk)

        @pl.when(j >= 3)
        def _():
            xz_ref[...] = acc.astype(bf16)


def _in_proj(x2, gain, w_main, w_dt, bsz, seq, tm, tk):
    rows = x2.shape[0]
    nl = seq // tm
    nj = OFF_DT // ATTN_WIDTH
    n_qkv, z_tile = OFF_Z // ATTN_WIDTH, OFF_Z // ATTN_WIDTH
    w_tile = lambda i, j: (0, jnp.where(j < n_qkv, j, jnp.where(j < nj - 1, j + 1, z_tile)))
    return pl.pallas_call(
        _in_proj_kernel,
        grid=(rows // tm, nj),
        in_specs=[
            pl.BlockSpec((tm, D_MODEL), lambda i, j: (i, 0)),
            pl.BlockSpec((1, D_MODEL), lambda i, j: (0, 0)),
            pl.BlockSpec((D_MODEL, ATTN_WIDTH), w_tile),
            pl.BlockSpec((D_MODEL, LANES), lambda i, j: (0, 0)),
        ],
        out_specs=[
            pl.BlockSpec((tm, ATTN_WIDTH), lambda i, j: (i, jnp.minimum(j, 1))),
            pl.BlockSpec((1, N_ATTN_HEADS, tm // tk, ATTN_V_DIM, tk), lambda i, j: (i // nl, 0, i % nl, 0, 0)),
            pl.BlockSpec((tm, ATTN_WIDTH), lambda i, j: (i, jnp.clip(j - 3, 0, 2))),
            pl.BlockSpec((tm, LANES), lambda i, j: (i, 0)),
        ],
        out_shape=[
            jax.ShapeDtypeStruct((rows, 2 * ATTN_WIDTH), bf16),
            jax.ShapeDtypeStruct((bsz, N_ATTN_HEADS, seq // tk, ATTN_V_DIM, tk), bf16),
            jax.ShapeDtypeStruct((rows, CONV_CH + SSM_WIDTH), bf16),
            jax.ShapeDtypeStruct((rows, LANES), f32),
        ],
        scratch_shapes=[pltpu.VMEM((tm, D_MODEL), bf16)],
        compiler_params=_cparams(2),
        name="in_proj",
    )(x2, gain, w_main, w_dt)


def _bias_block(tab_ref, h, delta):
    kk = lax.broadcasted_iota(jnp.int32, (MAX_DISTANCE, MAX_DISTANCE), 0)
    qq = lax.broadcasted_iota(jnp.int32, (MAX_DISTANCE, MAX_DISTANCE), 1)
    rel = delta * MAX_DISTANCE + kk - qq
    nb = NUM_BUCKETS // 2
    max_exact = nb // 2
    ret = jnp.where(rel > 0, nb, 0)
    n = jnp.abs(rel)
    nf = jnp.maximum(n, 1).astype(f32)
    large = max_exact + (jnp.log(nf / max_exact) / math.log(MAX_DISTANCE / max_exact)
                         * (nb - max_exact)).astype(jnp.int32)
    large = jnp.minimum(large, nb - 1)
    bucket = ret + jnp.where(n < max_exact, n, large)
    out = jnp.zeros(rel.shape, f32)
    for b in range(NUM_BUCKETS):
        out = jnp.where(bucket == b, tab_ref[b, h] * LOG2E, out)
    return out


def _bias_kernel(tab_ref, o_ref, *, tq, tk):
    h = pl.program_id(0)
    blk = MAX_DISTANCE
    side = {-2: jnp.full((blk, blk), tab_ref[NUM_BUCKETS // 2 - 1, h] * LOG2E, f32),
            2: jnp.full((blk, blk), tab_ref[NUM_BUCKETS - 1, h] * LOG2E, f32)}
    band = {delta: _bias_block(tab_ref, h, delta) for delta in (-1, 0, 1)}
    for d in range(tq // tk + 2):
        for a in range(tk // blk):
            for b in range(tq // blk):
                delta = (d - 1) * (tk // blk) + a - b
                val = band[delta] if abs(delta) <= 1 else side[2 if delta > 0 else -2]
                o_ref[0, d, a * blk:(a + 1) * blk, b * blk:(b + 1) * blk] = val


def _bias_tiles(rel_bias, tq, tk):
    n_near = tq // tk + 2
    return pl.pallas_call(
        functools.partial(_bias_kernel, tq=tq, tk=tk),
        grid=(N_ATTN_HEADS,),
        in_specs=[pl.BlockSpec(memory_space=pltpu.SMEM)],
        out_specs=pl.BlockSpec((1, n_near, tk, tq), lambda h: (h, 0, 0, 0)),
        out_shape=jax.ShapeDtypeStruct((N_ATTN_HEADS, n_near, tk, tq), f32),
        compiler_params=_cparams(1),
        name="bias_tiles",
    )(rel_bias)


def _attn_kernel(tab_ref, lam_ref, q_ref, k_ref, vt_ref, bias_ref, w_ref, o_ref,
                 qcat_ref, s_ref, p_ref, m_ref, acc_ref, *, tq, tk, nk, out_scale):
    h = pl.program_id(1)
    qi = pl.program_id(2)
    r = tq // tk
    c_left = tab_ref[NUM_BUCKETS // 2 - 1, h] * LOG2E
    c_right = tab_ref[NUM_BUCKETS - 1, h] * LOG2E
    lam = lam_ref[0]

    qt = q_ref[0].astype(f32).T
    row = lax.broadcasted_iota(jnp.int32, qt.shape, 0)
    qcat_ref[...] = jnp.concatenate([jnp.where(row < ATTN_HEAD_DIM, qt, 0.0),
                                     jnp.where(row >= ATTN_HEAD_DIM, qt, 0.0)], axis=1).astype(bf16)
    ones = jnp.ones((SUBLANES, tk), bf16)
    near_lo = jnp.maximum(qi * r - 1, 0)
    near_hi = jnp.minimum(qi * r + r + 1, nk)
    n_far = nk - (near_hi - near_lo)

    def logits(kt):
        return jnp.dot(k_ref[0, kt], qcat_ref[...], preferred_element_type=f32)

    def pv(kt, p):
        vext = jnp.concatenate([vt_ref[0, 0, kt], ones], axis=0)
        return jnp.dot(vext, p, preferred_element_type=f32)

    def tile_kt(t):
        far = t + jnp.where(t >= near_lo, near_hi - near_lo, 0)
        return jnp.where(t < n_far, far, near_lo + t - n_far)

    def tile_c(t):
        return jnp.where(t < n_far, jnp.where(t >= near_lo, c_right, c_left), 0.0)

    def stage_a(t, near):
        kt = tile_kt(t)
        s = logits(kt)
        if near:
            b = bias_ref[0, kt - qi * r + 1]
            s = s + jnp.concatenate([b, b], axis=1)
        s_ref[...] = s
        return jnp.max(s, axis=0, keepdims=True)

    def stage_b(t, smax):
        c = tile_c(t)
        m = m_ref[...]
        m_new = jnp.maximum(m, smax + c)
        p_ref[...] = jnp.exp2(s_ref[...] - (m_new - c)).astype(bf16)
        m_ref[...] = m_new
        return jnp.exp2(m - m_new)

    def stage_c(t, alpha):
        acc_ref[...] = acc_ref[...] * alpha + pv(tile_kt(t), p_ref[...])

    m_ref[...] = jnp.full(m_ref.shape, NEG, f32)
    acc_ref[...] = jnp.zeros(acc_ref.shape, f32)

    smax = stage_a(0, False)
    alpha = stage_b(0, smax)
    smax = stage_a(1, False)

    def step(t, carry, near):
        alpha, smax = carry
        stage_c(t - 2, alpha)
        alpha = stage_b(t - 1, smax)
        return alpha, stage_a(t, near)

    def run_steps(lo, hi, near, carry):
        first = lo + jnp.bitwise_and(hi - lo, TILE_UNROLL - 1)
        carry = lax.fori_loop(lo, first, lambda t, cr: step(t, cr, near), carry)

        def group(j, cr):
            for u in range(TILE_UNROLL):
                cr = step(first + TILE_UNROLL * j + u, cr, near)
            return cr

        n_groups = lax.shift_right_logical(hi - first, TILE_UNROLL.bit_length() - 1)
        return lax.fori_loop(0, n_groups, group, carry)

    carry = run_steps(2, n_far, False, (alpha, smax))
    alpha, smax = run_steps(n_far, nk, True, carry)
    stage_c(nk - 2, alpha)
    alpha = stage_b(nk - 1, smax)
    stage_c(nk - 1, alpha)

    acc = acc_ref[...]
    o = acc[:ATTN_V_DIM] * (1.0 / acc[ATTN_V_DIM:ATTN_V_DIM + 1])
    diff = (o[:, :tq] - lam * o[:, tq:]).T
    o_ref[0] = (_rms(diff, w_ref[...]) * out_scale).astype(bf16)


def _attention(qk3, vt, bias, rel_bias, lam, subln_w, lambda_init, tq, tk):
    bsz, seq = qk3.shape[0], qk3.shape[1]
    nk = seq // tk
    assert nk >= tq // tk + 4, "the far-tile pipeline needs at least two far key tiles per query tile"
    qk4 = qk3.reshape(bsz, nk, tk, 2 * ATTN_WIDTH)
    kern = functools.partial(_attn_kernel, tq=tq, tk=tk, nk=nk, out_scale=1.0 - lambda_init)
    return pl.pallas_call(
        kern,
        grid=(bsz, N_ATTN_HEADS, seq // tq),
        in_specs=[
            pl.BlockSpec(memory_space=pltpu.SMEM),
            pl.BlockSpec(memory_space=pltpu.SMEM),
            pl.BlockSpec((1, tq, ATTN_V_DIM), lambda b, h, i: (b, i, h)),
            pl.BlockSpec((1, nk, tk, ATTN_V_DIM), lambda b, h, i: (b, 0, 0, N_ATTN_HEADS + h)),
            pl.BlockSpec((1, 1, nk, ATTN_V_DIM, tk), lambda b, h, i: (b, h, 0, 0, 0)),
            pl.BlockSpec((1, tq // tk + 2, tk, tq), lambda b, h, i: (h, 0, 0, 0)),
            pl.BlockSpec((1, ATTN_V_DIM), lambda b, h, i: (0, 0)),
        ],
        out_specs=pl.BlockSpec((1, tq, ATTN_V_DIM), lambda b, h, i: (b, i, h)),
        out_shape=jax.ShapeDtypeStruct((bsz, seq, ATTN_WIDTH), bf16),
        scratch_shapes=[
            pltpu.VMEM((ATTN_V_DIM, 2 * tq), bf16),
            pltpu.VMEM((tk, 2 * tq), f32),
            pltpu.VMEM((tk, 2 * tq), bf16),
            pltpu.VMEM((1, 2 * tq), f32),
            pltpu.VMEM((ATTN_V_DIM + SUBLANES, 2 * tq), f32),
        ],
        compiler_params=_cparams(3),
        name="attention",
    )(rel_bias, lam, qk3, qk4, vt, bias, subln_w)


def _softplus(v):
    return jnp.maximum(v, 0.0) + jnp.log1p(jnp.exp(-jnp.abs(v)))


def _ssd_decay_terms(dt_ref, dtb_ref, alog_ref, tril_ref):
    dtv = _softplus(dt_ref[0] + dtb_ref[...])
    lane = lax.broadcasted_iota(jnp.int32, (1, LANES), 1)
    a_row = jnp.where(lane < 2 * N_SSM_HEADS, -jnp.exp(alog_ref[...]), 0.0)
    a = dtv * a_row
    cs = _dot3_left(tril_ref[...], a)
    return dtv, a, cs


def _ssd_fwd_kernel(cur_ref, prev_ref, next_ref, dt_ref, cw_ref, cb_ref, dtb_ref, alog_ref, dskip_ref,
                    tril_ref, triu_ref, ef_ref, shift_ref, y_ref, act_ref, state_ref):
    c = pl.program_id(1)
    nc = pl.num_programs(1)

    @pl.when(c == 0)
    def _():
        state_ref[...] = jnp.zeros_like(state_ref)

    pmask = (c > 0).astype(f32)
    nmask = (c < nc - 1).astype(f32)
    ext = jnp.concatenate([prev_ref[0] * pmask.astype(bf16), cur_ref[0], next_ref[0] * nmask.astype(bf16)],
                          axis=0)
    taps = jnp.dot(shift_ref[...], ext, preferred_element_type=f32)
    mid = D_CONV // 2
    conv = cur_ref[0].astype(f32) * cw_ref[mid:mid + 1, :] + cb_ref[...]
    for i, j in enumerate(_SIDE_TAPS):
        conv = conv + taps[i * CHUNK:(i + 1) * CHUNK, :] * cw_ref[j:j + 1, :]
    act = conv * (1.0 / (1.0 + jnp.exp(-conv)))
    actb = act.astype(bf16)
    act_ref[0] = actb

    dtv, a, cs = _ssd_decay_terms(dt_ref, dtb_ref, alog_ref, tril_ref)
    a_t = a.T
    dt_t = dtv.T
    cs_t = _dot3(a_t, triu_ref[...])
    ecs = cs - a
    ecs_t = cs_t - a_t

    tot = cs[CHUNK - 1:CHUNK, :]
    scale_off = _expand_heads(jnp.exp(cs), ef_ref[...], 0)
    scale_in = _expand_heads(dtv * jnp.exp(tot - cs), ef_ref[...], 0)
    carry_dec = scale_off[CHUNK - 1:CHUNK, :]

    li = lax.broadcasted_iota(jnp.int32, (CHUNK, CHUNK), 0)
    si = lax.broadcasted_iota(jnp.int32, (CHUNK, CHUNK), 1)
    lower = li >= si
    upper = si >= li

    for g in range(N_SSM_GROUPS):
        gs = slice(g * GROUP_W, (g + 1) * GROUP_W)
        bg = actb[:, SSM_WIDTH + g * D_STATE: SSM_WIDTH + (g + 1) * D_STATE]
        cg = actb[:, SSM_WIDTH + N_SSM_GROUPS * D_STATE + g * D_STATE:
                  SSM_WIDTH + N_SSM_GROUPS * D_STATE + (g + 1) * D_STATE]
        cbm = lax.dot_general(cg, bg, (((1,), (1,)), ((), ())), preferred_element_type=f32)
        ws, xs_blocks = [], []
        for hh in range(SSM_HPG):
            hd = g * SSM_HPG + hh
            hb = N_SSM_HEADS + hd
            dec_f = jnp.exp(jnp.where(lower, cs[:, hd:hd + 1] - cs_t[hd:hd + 1, :], NEG))
            dec_b = jnp.exp(jnp.where(upper, ecs_t[hb:hb + 1, :] - ecs[:, hb:hb + 1], NEG))
            w = cbm * (dec_f * dt_t[hd:hd + 1, :] + dec_b * dt_t[hb:hb + 1, :])
            ws.append(w.astype(bf16))
            lane = lax.broadcasted_iota(jnp.int32, (CHUNK, GROUP_W), 1)
            own = (lane >= hh * SSM_HEAD_DIM) & (lane < (hh + 1) * SSM_HEAD_DIM)
            xs_blocks.append(jnp.where(own, actb[:, gs], jnp.zeros((), bf16)))
        xg = act[:, gs]
        wcat = jnp.concatenate(ws, axis=1)
        xblk = jnp.concatenate(xs_blocks, axis=0)
        y = jnp.dot(wcat, xblk, preferred_element_type=f32)
        y = y + jnp.dot(cg, state_ref[g].astype(bf16), preferred_element_type=f32) * scale_off[:, gs]
        y_ref[0, :, gs] = y + xg * dskip_ref[:, gs]
        xs = (xg * scale_in[:, gs]).astype(bf16)
        new = jnp.dot(bg.astype(f32).T.astype(bf16), xs, preferred_element_type=f32)
        state_ref[g] = state_ref[g] * carry_dec[:, gs] + new


def _ssd_bwd_kernel(act_ref, dt_ref, y_ref, z_ref, dtb_ref, alog_ref, nw_ref, tril_ref, eb_ref,
                    o_ref, state_ref):
    c = pl.program_id(1)

    @pl.when(c == 0)
    def _():
        state_ref[...] = jnp.zeros_like(state_ref)

    actb = act_ref[0]
    dtv, a, cs = _ssd_decay_terms(dt_ref, dtb_ref, alog_ref, tril_ref)
    ecs = cs - a
    tot = cs[CHUNK - 1:CHUNK, :]
    scale_off = _expand_heads(jnp.exp(tot - ecs), eb_ref[...], N_SSM_HEADS)
    scale_in = _expand_heads(dtv * jnp.exp(ecs), eb_ref[...], N_SSM_HEADS)
    carry_dec = scale_off[0:1, :]

    zf = z_ref[0].astype(f32)
    gate = zf * (1.0 / (1.0 + jnp.exp(-zf)))
    for g in range(N_SSM_GROUPS):
        gs = slice(g * GROUP_W, (g + 1) * GROUP_W)
        bg = actb[:, SSM_WIDTH + g * D_STATE: SSM_WIDTH + (g + 1) * D_STATE]
        cg = actb[:, SSM_WIDTH + N_SSM_GROUPS * D_STATE + g * D_STATE:
                  SSM_WIDTH + N_SSM_GROUPS * D_STATE + (g + 1) * D_STATE]
        y = y_ref[0, :, gs] + (jnp.dot(cg, state_ref[g].astype(bf16), preferred_element_type=f32)
                               * scale_off[:, gs])
        xs = (actb[:, gs].astype(f32) * scale_in[:, gs]).astype(bf16)
        new = jnp.dot(bg.astype(f32).T.astype(bf16), xs, preferred_element_type=f32)
        state_ref[g] = state_ref[g] * carry_dec[:, gs] + new
        o_ref[0, :, gs] = _rms(y * gate[:, gs], nw_ref[:, gs]).astype(bf16)


def _const_spec(shape):
    return pl.BlockSpec(shape, lambda b, c: (0,) * len(shape))


def _ssd_fwd(xz3, dt3, conv_w, conv_b, dt_bias, a_log, d_skip, consts):
    bsz, seq = xz3.shape[0], xz3.shape[1]
    nc = seq // CHUNK
    hb = CHUNK // HALO
    nh = seq // HALO
    tril, triu, ef, _, shift = consts
    const = _const_spec
    return pl.pallas_call(
        _ssd_fwd_kernel,
        grid=(bsz, nc),
        in_specs=[
            pl.BlockSpec((1, CHUNK, CONV_CH), lambda b, c: (b, c, 0)),
            pl.BlockSpec((1, HALO, CONV_CH), lambda b, c: (b, jnp.maximum(c * hb - 1, 0), 0)),
            pl.BlockSpec((1, HALO, CONV_CH), lambda b, c: (b, jnp.minimum((c + 1) * hb, nh - 1), 0)),
            pl.BlockSpec((1, CHUNK, LANES), lambda b, c: (b, c, 0)),
            const((SUBLANES, CONV_CH)), const((1, CONV_CH)), const((1, LANES)), const((1, LANES)),
            const((1, SSM_WIDTH)), const((CHUNK, CHUNK)), const((CHUNK, CHUNK)), const((LANES, SSM_WIDTH)),
            const(((D_CONV - 1) * CHUNK, CHUNK + 2 * HALO)),
        ],
        out_specs=[
            pl.BlockSpec((1, CHUNK, SSM_WIDTH), lambda b, c: (b, c, 0)),
            pl.BlockSpec((1, CHUNK, CONV_CH), lambda b, c: (b, c, 0)),
        ],
        out_shape=[
            jax.ShapeDtypeStruct((bsz, seq, SSM_WIDTH), f32),
            jax.ShapeDtypeStruct((bsz, seq, CONV_CH), bf16),
        ],
        scratch_shapes=[pltpu.VMEM((N_SSM_GROUPS, D_STATE, GROUP_W), f32)],
        compiler_params=_cparams(2),
        name="ssd_fwd",
    )(xz3, xz3, xz3, dt3, conv_w, conv_b, dt_bias, a_log, d_skip, tril, triu, ef, shift)


def _ssd_bwd(act, dt3, y_part, xz3, dt_bias, a_log, norm_w, consts):
    bsz, seq = xz3.shape[0], xz3.shape[1]
    nc = seq // CHUNK
    tril, _, _, eb, _ = consts
    const = _const_spec
    rev = lambda b, c: (b, nc - 1 - c, 0)
    return pl.pallas_call(
        _ssd_bwd_kernel,
        grid=(bsz, nc),
        in_specs=[
            pl.BlockSpec((1, CHUNK, CONV_CH), rev),
            pl.BlockSpec((1, CHUNK, LANES), rev),
            pl.BlockSpec((1, CHUNK, SSM_WIDTH), rev),
            pl.BlockSpec((1, CHUNK, SSM_WIDTH), lambda b, c: (b, nc - 1 - c, CONV_CH // SSM_WIDTH)),
            const((1, LANES)), const((1, LANES)), const((1, SSM_WIDTH)),
            const((CHUNK, CHUNK)), const((LANES, SSM_WIDTH)),
        ],
        out_specs=pl.BlockSpec((1, CHUNK, SSM_WIDTH), rev),
        out_shape=jax.ShapeDtypeStruct((bsz, seq, SSM_WIDTH), bf16),
        scratch_shapes=[pltpu.VMEM((N_SSM_GROUPS, D_STATE, GROUP_W), f32)],
        compiler_params=_cparams(2),
        name="ssd_bwd",
    )(act, dt3, y_part, xz3, dt_bias, a_log, norm_w, tril, eb)


def _ssd(xz3, dt3, conv_w, conv_b, dt_bias, a_log, d_skip, norm_w, consts):
    y_part, act = _ssd_fwd(xz3, dt3, conv_w, conv_b, dt_bias, a_log, d_skip, consts)
    return _ssd_bwd(act, dt3, y_part, xz3, dt_bias, a_log, norm_w, consts)


def _ssd_consts():
    idx = np.arange(CHUNK)
    tril = (idx[None, :] <= idx[:, None]).astype(np.float32)
    head = np.arange(SSM_WIDTH) // SSM_HEAD_DIM
    rows = np.arange(LANES)
    piece = lambda first: (rows >= first) & (rows < first + 3 * N_SSM_HEADS)
    ef = (piece(0)[:, None] & (rows[:, None] % N_SSM_HEADS == head[None, :])).astype(np.float32)
    eb = (piece(N_SSM_HEADS)[:, None] & (rows[:, None] % N_SSM_HEADS == head[None, :])).astype(np.float32)
    out_row = np.arange((D_CONV - 1) * CHUNK)
    tap = np.asarray(_SIDE_TAPS)[out_row // CHUNK]
    src = HALO + out_row % CHUNK + tap - D_CONV // 2
    shift = (src[:, None] == np.arange(CHUNK + 2 * HALO)[None, :]).astype(np.float32)
    return tuple(jnp.asarray(m, bf16) for m in (tril, tril.T, ef, eb, shift))


def _out_proj_kernel(a_ref, s_ref, w_ref, x_ref, g_ref, o_ref):
    mix = jnp.dot(a_ref[...], w_ref[:ATTN_WIDTH, :], preferred_element_type=f32)
    mix = mix + jnp.dot(s_ref[...], w_ref[ATTN_WIDTH:, :], preferred_element_type=f32)
    o_ref[...] = x_ref[...] + _rms(mix, g_ref[...])


def _out_proj(a2, s2, w_out, x2, gain, tm):
    rows = x2.shape[0]
    return pl.pallas_call(
        _out_proj_kernel,
        grid=(rows // tm,),
        in_specs=[
            pl.BlockSpec((tm, ATTN_WIDTH), lambda i: (i, 0)),
            pl.BlockSpec((tm, SSM_WIDTH), lambda i: (i, 0)),
            pl.BlockSpec((D_MODEL, D_MODEL), lambda i: (0, 0)),
            pl.BlockSpec((tm, D_MODEL), lambda i: (i, 0)),
            pl.BlockSpec((1, D_MODEL), lambda i: (0, 0)),
        ],
        out_specs=pl.BlockSpec((tm, D_MODEL), lambda i: (i, 0)),
        out_shape=jax.ShapeDtypeStruct((rows, D_MODEL), f32),
        compiler_params=_cparams(1),
        name="out_proj",
    )(a2, s2, w_out, x2, gain)


def _mlp_kernel(x_ref, gpre_ref, wup_ref, wdn_ref, gpost_ref, o_ref, h_ref):
    j = pl.program_id(1)
    last = pl.num_programs(1) - 1
    half = x_ref.shape[0] // 2
    halves = (slice(0, half), slice(half, 2 * half))

    def ff(h):
        u = jnp.dot(h, wup_ref[...], preferred_element_type=f32)
        u = jnp.square(jnp.maximum(u, 0.0)).astype(bf16)
        return jnp.dot(u, wdn_ref[...], preferred_element_type=f32)

    @pl.when(j == 0)
    def _():
        for rows in halves:
            h = _rms(x_ref[rows], gpre_ref[...]).astype(bf16)
            h_ref[rows] = h
            o_ref[rows] = ff(h)

    @pl.when((j > 0) & (j < last))
    def _():
        o_ref[...] += ff(h_ref[...])

    @pl.when(j == last)
    def _():
        for rows in halves:
            y = o_ref[rows] + ff(h_ref[rows])
            o_ref[rows] = x_ref[rows] + _rms(y, gpost_ref[...])


def _mlp(x2, g_pre, w_up, w_down, g_post, tm, tf):
    rows = x2.shape[0]
    assert D_FF // tf >= 2, "the first and the last d_ff step are distinct code paths"
    return pl.pallas_call(
        _mlp_kernel,
        grid=(rows // tm, D_FF // tf),
        in_specs=[
            pl.BlockSpec((tm, D_MODEL), lambda i, j: (i, 0)),
            pl.BlockSpec((1, D_MODEL), lambda i, j: (0, 0)),
            pl.BlockSpec((D_MODEL, tf), lambda i, j: (0, j)),
            pl.BlockSpec((tf, D_MODEL), lambda i, j: (j, 0)),
            pl.BlockSpec((1, D_MODEL), lambda i, j: (0, 0)),
        ],
        out_specs=pl.BlockSpec((tm, D_MODEL), lambda i, j: (i, 0)),
        out_shape=jax.ShapeDtypeStruct((rows, D_MODEL), f32),
        scratch_shapes=[pltpu.VMEM((tm, D_MODEL), bf16)],
        compiler_params=_cparams(2),
        name="mlp",
    )(x2, g_pre, w_up, w_down, g_post)


def _attn_tiles(seq):
    for tq, tk in ((1024, 512), (512, 512), (256, 256)):
        if seq % tq == 0 and seq // tk >= tq // tk + 4:
            return tq, tk
    raise ValueError(f"sequence length {seq} is too short for the attention tiling")


def _trunk(x, layers, rel_bias, bias, consts, tq, tk):
    bsz, seq = x.shape[0], x.shape[1]
    rows = bsz * seq
    tm = tk
    tm_wide = ROW_TILE if seq % ROW_TILE == 0 and ROW_TILE % tk == 0 else tm
    x2 = x.reshape(rows, D_MODEL)
    for lp in layers:
        qk, vt, xz, dt = _in_proj(x2, lp["pre_norm_mix"], lp["w_main"], lp["w_dt"], bsz, seq, tm_wide, tk)
        a_out = _attention(qk.reshape(bsz, seq, 2 * ATTN_WIDTH), vt, bias, rel_bias, lp["lam"], lp["attn_norm"],
                           lp["lambda_init"], tq, tk)
        s_out = _ssd(xz.reshape(bsz, seq, CONV_CH + SSM_WIDTH), dt.reshape(bsz, seq, LANES),
                     lp["conv_w"], lp["conv_b"], lp["dt_bias"], lp["a_log"], lp["d_skip"], lp["ssm_norm"],
                     consts)
        x2 = _out_proj(a_out.reshape(rows, ATTN_WIDTH), s_out.reshape(rows, SSM_WIDTH), lp["w_out"], x2,
                       lp["post_norm_mix"], tm)
        x2 = _mlp(x2, lp["pre_norm_mlp"], lp["w_up"], lp["w_down"], lp["post_norm_mlp"], tm_wide, FF_TILE)
    return x2.reshape(bsz, seq, D_MODEL)


def _pad_lanes(v, width=LANES):
    return jnp.pad(v, (0, width - v.shape[0]))[None, :].astype(f32)


def kernel(x_prompt, x_sample, rel_bias, pre_norm_mix, w_in, lambda_q1, lambda_k1, lambda_q2, lambda_k2,
           attn_norm, conv_w, conv_b, dt_bias_fwd, dt_bias_bwd, a_log_fwd, a_log_bwd, d_skip, ssm_norm,
           w_out, post_norm_mix, pre_norm_mlp, w_up, w_down, post_norm_mlp):
    depth = w_in.shape[0]
    row = lambda v: v[None, :].astype(f32)
    layers = []
    for i in range(depth):
        lambda_init = 0.8 - 0.6 * math.exp(-0.3 * i)
        lam = (jnp.exp(jnp.sum(lambda_q1[i].astype(f32) * lambda_k1[i].astype(f32)))
               - jnp.exp(jnp.sum(lambda_q2[i].astype(f32) * lambda_k2[i].astype(f32))) + lambda_init)
        wi = w_in[i]
        w_main = wi.astype(bf16)
        w_dt = jnp.pad(wi[:, OFF_DT:], ((0, 0), (0, LANES - 2 * N_SSM_HEADS))).astype(bf16)
        layers.append(dict(
            lambda_init=lambda_init,
            lam=lam.reshape(1).astype(f32),
            pre_norm_mix=row(pre_norm_mix[i]), w_main=w_main, w_dt=w_dt,
            attn_norm=row(attn_norm[i]),
            conv_w=jnp.pad(conv_w[i].astype(f32), ((0, SUBLANES - D_CONV), (0, 0))), conv_b=row(conv_b[i]),
            dt_bias=_pad_lanes(jnp.concatenate([dt_bias_fwd[i], dt_bias_bwd[i]])),
            a_log=_pad_lanes(jnp.concatenate([a_log_fwd[i], a_log_bwd[i]])),
            d_skip=row(jnp.repeat(d_skip[i], SSM_HEAD_DIM)), ssm_norm=row(ssm_norm[i]),
            w_out=w_out[i].astype(bf16), post_norm_mix=row(post_norm_mix[i]),
            pre_norm_mlp=row(pre_norm_mlp[i]), w_up=w_up[i].astype(bf16), w_down=w_down[i].astype(bf16),
            post_norm_mlp=row(post_norm_mlp[i]),
        ))
    consts = _ssd_consts()
    rel = rel_bias.astype(f32)
    outs = []
    biases = {}
    for x in (x_prompt, x_sample):
        tq, tk = _attn_tiles(x.shape[1])
        if (tq, tk) not in biases:
            biases[(tq, tk)] = _bias_tiles(rel, tq, tk)
        outs.append(_trunk(x, layers, rel, biases[(tq, tk)], consts, tq, tk))
    return tuple(outs)
```

```python
import functools
import math

import jax
import jax.numpy as jnp
import numpy as np
from jax import lax
from jax.experimental import pallas as pl
from jax.experimental.pallas import tpu as pltpu

f32 = jnp.float32
bf16 = jnp.bfloat16

D_MODEL = 2048
N_ATTN_HEADS = 8
ATTN_HEAD_DIM = 64
ATTN_V_DIM = 2 * ATTN_HEAD_DIM
ATTN_WIDTH = N_ATTN_HEADS * ATTN_V_DIM
SSM_WIDTH = 1024
SSM_HEAD_DIM = 64
N_SSM_HEADS = 16
N_SSM_GROUPS = 4
SSM_HPG = 4
D_STATE = 128
D_CONV = 5
CHUNK = 128
CONV_CH = SSM_WIDTH + 2 * N_SSM_GROUPS * D_STATE
GROUP_W = SSM_HPG * SSM_HEAD_DIM
OFF_Q, OFF_K, OFF_V, OFF_Z = 0, 1024, 2048, 3072
OFF_XBC = 4096
OFF_DT = OFF_XBC + CONV_CH
D_FF = 4 * D_MODEL
NUM_BUCKETS = 32
MAX_DISTANCE = 128
EPS = 1e-6
NEG = -1e30
LOG2E = math.log2(math.e)

LANES = 128
SUBLANES = 8
ROW_TILE = 1024
FF_TILE = 512
MLP_EDGE_PARTS = 2
TILE_UNROLL = 2
HALO = 16
_SIDE_TAPS = tuple(j for j in range(D_CONV) if j != D_CONV // 2)
VMEM_LIMIT = 56 * 1024 * 1024


def _cparams(n_axes):
    return pltpu.CompilerParams(dimension_semantics=("arbitrary",) * n_axes,
                                vmem_limit_bytes=VMEM_LIMIT)


def _rms(xf, g):
    ms = jnp.mean(xf * xf, axis=-1, keepdims=True)
    return (xf * lax.rsqrt(ms + EPS)) * g


def _split3(v):
    hi = v.astype(bf16)
    r1 = v - hi.astype(f32)
    mid = r1.astype(bf16)
    lo = (r1 - mid.astype(f32)).astype(bf16)
    return hi, mid, lo


def _dot3(v, m01):
    hi, mid, lo = _split3(v)
    d = lambda a: jnp.dot(a, m01, preferred_element_type=f32)
    return d(hi) + d(mid) + d(lo)


def _expand_heads(v, e3, first):
    lane = lax.broadcasted_iota(jnp.int32, v.shape, 1)
    hi, mid, lo = _split3(jnp.where((lane >= first) & (lane < first + N_SSM_HEADS), v, 0.0))
    packed = (hi.astype(f32) + pltpu.roll(mid.astype(f32), N_SSM_HEADS, 1)
              + pltpu.roll(lo.astype(f32), 2 * N_SSM_HEADS, 1))
    return jnp.dot(packed.astype(bf16), e3, preferred_element_type=f32)


def _dot3_left(m01, v):
    hi, mid, lo = _split3(v)
    d = lambda a: jnp.dot(m01, a, preferred_element_type=f32)
    return d(hi) + d(mid) + d(lo)


def _in_proj_kernel(x_ref, g_ref, w_ref, wdt_ref, qk_ref, vt_ref, xz_ref, dt_ref, h_ref):
    j = pl.program_id(1)
    half = x_ref.shape[0] // 2

    @pl.when(j == 0)
    def _():
        for rows in (slice(0, half), slice(half, 2 * half)):
            hb = _rms(x_ref[rows], g_ref[...]).astype(bf16)
            h_ref[rows] = hb
            dt_ref[rows] = jnp.dot(hb, wdt_ref[...], preferred_element_type=f32)
            q = jnp.dot(hb, w_ref[...], preferred_element_type=f32)
            qk_ref[rows] = (q * (ATTN_HEAD_DIM ** -0.5 * LOG2E)).astype(bf16)

    @pl.when(j == 2)
    def _():
        tk = vt_ref.shape[-1]
        for u in range(vt_ref.shape[2]):
            v = jnp.dot(h_ref[u * tk:(u + 1) * tk], w_ref[...], preferred_element_type=f32)
            vt_ref[0, :, u] = v.T.astype(bf16).reshape(N_ATTN_HEADS, ATTN_V_DIM, tk)

    @pl.when((j == 1) | (j >= 3))
    def _():
        acc = jnp.dot(h_ref[...], w_ref[...], preferred_element_type=f32)

        @pl.when(j == 1)
        def _():
            qk_ref[...] = acc.astype(bf16)

        @pl.when(j >= 3)
        def _():
            xz_ref[...] = acc.astype(bf16)


def _in_proj(x2, gain, w_main, w_dt, bsz, seq, tm, tk):
    rows = x2.shape[0]
    nl = seq // tm
    nj = OFF_DT // ATTN_WIDTH
    n_qkv, z_tile = OFF_Z // ATTN_WIDTH, OFF_Z // ATTN_WIDTH
    w_tile = lambda i, j: (0, jnp.where(j < n_qkv, j, jnp.where(j < nj - 1, j + 1, z_tile)))
    return pl.pallas_call(
        _in_proj_kernel,
        grid=(rows // tm, nj),
        in_specs=[
            pl.BlockSpec((tm, D_MODEL), lambda i, j: (i, 0)),
            pl.BlockSpec((1, D_MODEL), lambda i, j: (0, 0)),
            pl.BlockSpec((D_MODEL, ATTN_WIDTH), w_tile),
            pl.BlockSpec((D_MODEL, LANES), lambda i, j: (0, 0)),
        ],
        out_specs=[
            pl.BlockSpec((tm, ATTN_WIDTH), lambda i, j: (i, jnp.minimum(j, 1))),
            pl.BlockSpec((1, N_ATTN_HEADS, tm // tk, ATTN_V_DIM, tk), lambda i, j: (i // nl, 0, i % nl, 0, 0)),
            pl.BlockSpec((tm, ATTN_WIDTH), lambda i, j: (i, jnp.clip(j - 3, 0, 2))),
            pl.BlockSpec((tm, LANES), lambda i, j: (i, 0)),
        ],
        out_shape=[
            jax.ShapeDtypeStruct((rows, 2 * ATTN_WIDTH), bf16),
            jax.ShapeDtypeStruct((bsz, N_ATTN_HEADS, seq // tk, ATTN_V_DIM, tk), bf16),
            jax.ShapeDtypeStruct((rows, CONV_CH + SSM_WIDTH), bf16),
            jax.ShapeDtypeStruct((rows, LANES), f32),
        ],
        scratch_shapes=[pltpu.VMEM((tm, D_MODEL), bf16)],
        compiler_params=_cparams(2),
        name="in_proj",
    )(x2, gain, w_main, w_dt)


def _bias_block(tab_ref, h, delta):
    kk = lax.broadcasted_iota(jnp.int32, (MAX_DISTANCE, MAX_DISTANCE), 0)
    qq = lax.broadcasted_iota(jnp.int32, (MAX_DISTANCE, MAX_DISTANCE), 1)
    rel = delta * MAX_DISTANCE + kk - qq
    nb = NUM_BUCKETS // 2
    max_exact = nb // 2
    ret = jnp.where(rel > 0, nb, 0)
    n = jnp.abs(rel)
    nf = jnp.maximum(n, 1).astype(f32)
    large = max_exact + (jnp.log(nf / max_exact) / math.log(MAX_DISTANCE / max_exact)
                         * (nb - max_exact)).astype(jnp.int32)
    large = jnp.minimum(large, nb - 1)
    bucket = ret + jnp.where(n < max_exact, n, large)
    out = jnp.zeros(rel.shape, f32)
    for b in range(NUM_BUCKETS):
        out = jnp.where(bucket == b, tab_ref[b, h] * LOG2E, out)
    return out


def _bias_kernel(tab_ref, o_ref, *, tq, tk):
    h = pl.program_id(0)
    blk = MAX_DISTANCE
    side = {-2: jnp.full((blk, blk), tab_ref[NUM_BUCKETS // 2 - 1, h] * LOG2E, f32),
            2: jnp.full((blk, blk), tab_ref[NUM_BUCKETS - 1, h] * LOG2E, f32)}
    band = {delta: _bias_block(tab_ref, h, delta) for delta in (-1, 0, 1)}
    for d in range(tq // tk + 2):
        for a in range(tk // blk):
            for b in range(tq // blk):
                delta = (d - 1) * (tk // blk) + a - b
                val = band[delta] if abs(delta) <= 1 else side[2 if delta > 0 else -2]
                o_ref[0, d, a * blk:(a + 1) * blk, b * blk:(b + 1) * blk] = val


def _bias_tiles(rel_bias, tq, tk):
    n_near = tq // tk + 2
    return pl.pallas_call(
        functools.partial(_bias_kernel, tq=tq, tk=tk),
        grid=(N_ATTN_HEADS,),
        in_specs=[pl.BlockSpec(memory_space=pltpu.SMEM)],
        out_specs=pl.BlockSpec((1, n_near, tk, tq), lambda h: (h, 0, 0, 0)),
        out_shape=jax.ShapeDtypeStruct((N_ATTN_HEADS, n_near, tk, tq), f32),
        compiler_params=_cparams(1),
        name="bias_tiles",
    )(rel_bias)


def _attn_kernel(tab_ref, lam_ref, q_ref, k_ref, vt_ref, bias_ref, w_ref, o_ref,
                 qcat_ref, s_ref, p_ref, m_ref, acc_ref, *, tq, tk, nk, out_scale):
    h = pl.program_id(1)
    qi = pl.program_id(2)
    r = tq // tk
    c_left = tab_ref[NUM_BUCKETS // 2 - 1, h] * LOG2E
    c_right = tab_ref[NUM_BUCKETS - 1, h] * LOG2E
    lam = lam_ref[0]

    qt = q_ref[0].astype(f32).T
    row = lax.broadcasted_iota(jnp.int32, qt.shape, 0)
    qcat_ref[...] = jnp.concatenate([jnp.where(row < ATTN_HEAD_DIM, qt, 0.0),
                                     jnp.where(row >= ATTN_HEAD_DIM, qt, 0.0)], axis=1).astype(bf16)
    ones = jnp.ones((SUBLANES, tk), bf16)
    near_lo = jnp.maximum(qi * r - 1, 0)
    near_hi = jnp.minimum(qi * r + r + 1, nk)
    n_far = nk - (near_hi - near_lo)

    def logits(kt):
        return jnp.dot(k_ref[0, kt], qcat_ref[...], preferred_element_type=f32)

    def pv(kt, p):
        vext = jnp.concatenate([vt_ref[0, 0, kt], ones], axis=0)
        return jnp.dot(vext, p, preferred_element_type=f32)

    def tile_kt(t):
        far = t + jnp.where(t >= near_lo, near_hi - near_lo, 0)
        return jnp.where(t < n_far, far, near_lo + t - n_far)

    def tile_c(t):
        return jnp.where(t < n_far, jnp.where(t >= near_lo, c_right, c_left), 0.0)

    def stage_a(t, near):
        kt = tile_kt(t)
        s = logits(kt)
        if near:
            b = bias_ref[0, kt - qi * r + 1]
            s = s + jnp.concatenate([b, b], axis=1)
        s_ref[...] = s
        return jnp.max(s, axis=0, keepdims=True)

    def stage_b(t, smax):
        c = tile_c(t)
        m = m_ref[...]
        m_new = jnp.maximum(m, smax + c)
        p_ref[...] = jnp.exp2(s_ref[...] - (m_new - c)).astype(bf16)
        m_ref[...] = m_new
        return jnp.exp2(m - m_new)

    def stage_c(t, alpha):
        acc_ref[...] = acc_ref[...] * alpha + pv(tile_kt(t), p_ref[...])

    m_ref[...] = jnp.full(m_ref.shape, NEG, f32)
    acc_ref[...] = jnp.zeros(acc_ref.shape, f32)

    smax = stage_a(0, False)
    alpha = stage_b(0, smax)
    smax = stage_a(1, False)

    def step(t, carry, near):
        alpha, smax = carry
        stage_c(t - 2, alpha)
        alpha = stage_b(t - 1, smax)
        return alpha, stage_a(t, near)

    def run_steps(lo, hi, near, carry):
        first = lo + jnp.bitwise_and(hi - lo, TILE_UNROLL - 1)
        carry = lax.fori_loop(lo, first, lambda t, cr: step(t, cr, near), carry)

        def group(j, cr):
            for u in range(TILE_UNROLL):
                cr = step(first + TILE_UNROLL * j + u, cr, near)
            return cr

        n_groups = lax.shift_right_logical(hi - first, TILE_UNROLL.bit_length() - 1)
        return lax.fori_loop(0, n_groups, group, carry)

    carry = run_steps(2, n_far, False, (alpha, smax))
    alpha, smax = run_steps(n_far, nk, True, carry)
    stage_c(nk - 2, alpha)
    alpha = stage_b(nk - 1, smax)
    stage_c(nk - 1, alpha)

    acc = acc_ref[...]
    o = acc[:ATTN_V_DIM] * (1.0 / acc[ATTN_V_DIM:ATTN_V_DIM + 1])
    diff = (o[:, :tq] - lam * o[:, tq:]).T
    o_ref[0] = (_rms(diff, w_ref[...]) * out_scale).astype(bf16)


def _attention(qk3, vt, bias, rel_bias, lam, subln_w, lambda_init, tq, tk):
    bsz, seq = qk3.shape[0], qk3.shape[1]
    nk = seq // tk
    assert nk >= tq // tk + 4, "the far-tile pipeline needs at least two far key tiles per query tile"
    qk4 = qk3.reshape(bsz, nk, tk, 2 * ATTN_WIDTH)
    kern = functools.partial(_attn_kernel, tq=tq, tk=tk, nk=nk, out_scale=1.0 - lambda_init)
    return pl.pallas_call(
        kern,
        grid=(bsz, N_ATTN_HEADS, seq // tq),
        in_specs=[
            pl.BlockSpec(memory_space=pltpu.SMEM),
            pl.BlockSpec(memory_space=pltpu.SMEM),
            pl.BlockSpec((1, tq, ATTN_V_DIM), lambda b, h, i: (b, i, h)),
            pl.BlockSpec((1, nk, tk, ATTN_V_DIM), lambda b, h, i: (b, 0, 0, N_ATTN_HEADS + h)),
            pl.BlockSpec((1, 1, nk, ATTN_V_DIM, tk), lambda b, h, i: (b, h, 0, 0, 0)),
            pl.BlockSpec((1, tq // tk + 2, tk, tq), lambda b, h, i: (h, 0, 0, 0)),
            pl.BlockSpec((1, ATTN_V_DIM), lambda b, h, i: (0, 0)),
        ],
        out_specs=pl.BlockSpec((1, tq, ATTN_V_DIM), lambda b, h, i: (b, i, h)),
        out_shape=jax.ShapeDtypeStruct((bsz, seq, ATTN_WIDTH), bf16),
        scratch_shapes=[
            pltpu.VMEM((ATTN_V_DIM, 2 * tq), bf16),
            pltpu.VMEM((tk, 2 * tq), f32),
            pltpu.VMEM((tk, 2 * tq), bf16),
            pltpu.VMEM((1, 2 * tq), f32),
            pltpu.VMEM((ATTN_V_DIM + SUBLANES, 2 * tq), f32),
        ],
        compiler_params=_cparams(3),
        name="attention",
    )(rel_bias, lam, qk3, qk4, vt, bias, subln_w)


def _softplus(v):
    return jnp.maximum(v, 0.0) + jnp.log1p(jnp.exp(-jnp.abs(v)))


def _ssd_decay_terms(dt_ref, dtb_ref, alog_ref, tril_ref):
    dtv = _softplus(dt_ref[0] + dtb_ref[...])
    lane = lax.broadcasted_iota(jnp.int32, (1, LANES), 1)
    a_row = jnp.where(lane < 2 * N_SSM_HEADS, -jnp.exp(alog_ref[...]), 0.0)
    a = dtv * a_row
    cs = _dot3_left(tril_ref[...], a)
    return dtv, a, cs


def _ssd_fwd_kernel(cur_ref, prev_ref, next_ref, dt_ref, cw_ref, cb_ref, dtb_ref, alog_ref, dskip_ref,
                    tril_ref, triu_ref, ef_ref, shift_ref, y_ref, act_ref, state_ref):
    c = pl.program_id(1)
    nc = pl.num_programs(1)

    @pl.when(c == 0)
    def _():
        state_ref[...] = jnp.zeros_like(state_ref)

    pmask = (c > 0).astype(f32)
    nmask = (c < nc - 1).astype(f32)
    ext = jnp.concatenate([prev_ref[0] * pmask.astype(bf16), cur_ref[0], next_ref[0] * nmask.astype(bf16)],
                          axis=0)
    taps = jnp.dot(shift_ref[...], ext, preferred_element_type=f32)
    mid = D_CONV // 2
    conv = cur_ref[0].astype(f32) * cw_ref[mid:mid + 1, :] + cb_ref[...]
    for i, j in enumerate(_SIDE_TAPS):
        conv = conv + taps[i * CHUNK:(i + 1) * CHUNK, :] * cw_ref[j:j + 1, :]
    act = conv * (1.0 / (1.0 + jnp.exp(-conv)))
    actb = act.astype(bf16)
    act_ref[0] = actb

    dtv, a, cs = _ssd_decay_terms(dt_ref, dtb_ref, alog_ref, tril_ref)
    a_t = a.T
    dt_t = dtv.T
    cs_t = _dot3(a_t, triu_ref[...])
    ecs = cs - a
    ecs_t = cs_t - a_t

    tot = cs[CHUNK - 1:CHUNK, :]
    scale_off = _expand_heads(jnp.exp(cs), ef_ref[...], 0)
    scale_in = _expand_heads(dtv * jnp.exp(tot - cs), ef_ref[...], 0)
    carry_dec = scale_off[CHUNK - 1:CHUNK, :]

    log2_dt_t = jnp.log(dt_t) * LOG2E
    col_f, row_f = cs * LOG2E, cs_t * LOG2E - log2_dt_t
    col_b, row_b = ecs * LOG2E, ecs_t * LOG2E + log2_dt_t
    li = lax.broadcasted_iota(jnp.int32, (CHUNK, CHUNK), 0)
    si = lax.broadcasted_iota(jnp.int32, (CHUNK, CHUNK), 1)
    lower = li >= si
    upper = si >= li

    for g in range(N_SSM_GROUPS):
        gs = slice(g * GROUP_W, (g + 1) * GROUP_W)
        bg = actb[:, SSM_WIDTH + g * D_STATE: SSM_WIDTH + (g + 1) * D_STATE]
        cg = actb[:, SSM_WIDTH + N_SSM_GROUPS * D_STATE + g * D_STATE:
                  SSM_WIDTH + N_SSM_GROUPS * D_STATE + (g + 1) * D_STATE]
        cbm = lax.dot_general(cg, bg, (((1,), (1,)), ((), ())), preferred_element_type=f32)
        ws, xs_blocks = [], []
        for hh in range(SSM_HPG):
            hd = g * SSM_HPG + hh
            hb = N_SSM_HEADS + hd
            dec_f = jnp.exp2(jnp.where(lower, col_f[:, hd:hd + 1] - row_f[hd:hd + 1, :], NEG))
            dec_b = jnp.exp2(jnp.where(upper, row_b[hb:hb + 1, :] - col_b[:, hb:hb + 1], NEG))
            ws.append((cbm * (dec_f + dec_b)).astype(bf16))
            lane = lax.broadcasted_iota(jnp.int32, (CHUNK, GROUP_W), 1)
            own = (lane >= hh * SSM_HEAD_DIM) & (lane < (hh + 1) * SSM_HEAD_DIM)
            xs_blocks.append(jnp.where(own, actb[:, gs], jnp.zeros((), bf16)))
        xg = act[:, gs]
        wcat = jnp.concatenate(ws, axis=1)
        xblk = jnp.concatenate(xs_blocks, axis=0)
        y = jnp.dot(wcat, xblk, preferred_element_type=f32)
        y = y + jnp.dot(cg, state_ref[g].astype(bf16), preferred_element_type=f32) * scale_off[:, gs]
        y_ref[0, :, gs] = y + xg * dskip_ref[:, gs]
        xs = (xg * scale_in[:, gs]).astype(bf16)
        new = jnp.dot(bg.astype(f32).T.astype(bf16), xs, preferred_element_type=f32)
        state_ref[g] = state_ref[g] * carry_dec[:, gs] + new


def _ssd_bwd_kernel(act_ref, dt_ref, y_ref, z_ref, dtb_ref, alog_ref, nw_ref, tril_ref, eb_ref,
                    o_ref, state_ref):
    c = pl.program_id(1)

    @pl.when(c == 0)
    def _():
        state_ref[...] = jnp.zeros_like(state_ref)

    actb = act_ref[0]
    dtv, a, cs = _ssd_decay_terms(dt_ref, dtb_ref, alog_ref, tril_ref)
    ecs = cs - a
    tot = cs[CHUNK - 1:CHUNK, :]
    scale_off = _expand_heads(jnp.exp(tot - ecs), eb_ref[...], N_SSM_HEADS)
    scale_in = _expand_heads(dtv * jnp.exp(ecs), eb_ref[...], N_SSM_HEADS)
    carry_dec = scale_off[0:1, :]

    zf = z_ref[0].astype(f32)
    gate = zf * (1.0 / (1.0 + jnp.exp(-zf)))
    for g in range(N_SSM_GROUPS):
        gs = slice(g * GROUP_W, (g + 1) * GROUP_W)
        bg = actb[:, SSM_WIDTH + g * D_STATE: SSM_WIDTH + (g + 1) * D_STATE]
        cg = actb[:, SSM_WIDTH + N_SSM_GROUPS * D_STATE + g * D_STATE:
                  SSM_WIDTH + N_SSM_GROUPS * D_STATE + (g + 1) * D_STATE]
        y = y_ref[0, :, gs] + (jnp.dot(cg, state_ref[g].astype(bf16), preferred_element_type=f32)
                               * scale_off[:, gs])
        xs = (actb[:, gs].astype(f32) * scale_in[:, gs]).astype(bf16)
        new = jnp.dot(bg.astype(f32).T.astype(bf16), xs, preferred_element_type=f32)
        state_ref[g] = state_ref[g] * carry_dec[:, gs] + new
        o_ref[0, :, gs] = _rms(y * gate[:, gs], nw_ref[:, gs]).astype(bf16)


def _const_spec(shape):
    return pl.BlockSpec(shape, lambda b, c: (0,) * len(shape))


def _ssd_fwd(xz3, dt3, conv_w, conv_b, dt_bias, a_log, d_skip, consts):
    bsz, seq = xz3.shape[0], xz3.shape[1]
    nc = seq // CHUNK
    hb = CHUNK // HALO
    nh = seq // HALO
    tril, triu, ef, _, shift = consts
    const = _const_spec
    return pl.pallas_call(
        _ssd_fwd_kernel,
        grid=(bsz, nc),
        in_specs=[
            pl.BlockSpec((1, CHUNK, CONV_CH), lambda b, c: (b, c, 0)),
            pl.BlockSpec((1, HALO, CONV_CH), lambda b, c: (b, jnp.maximum(c * hb - 1, 0), 0)),
            pl.BlockSpec((1, HALO, CONV_CH), lambda b, c: (b, jnp.minimum((c + 1) * hb, nh - 1), 0)),
            pl.BlockSpec((1, CHUNK, LANES), lambda b, c: (b, c, 0)),
            const((SUBLANES, CONV_CH)), const((1, CONV_CH)), const((1, LANES)), const((1, LANES)),
            const((1, SSM_WIDTH)), const((CHUNK, CHUNK)), const((CHUNK, CHUNK)), const((LANES, SSM_WIDTH)),
            const(((D_CONV - 1) * CHUNK, CHUNK + 2 * HALO)),
        ],
        out_specs=[
            pl.BlockSpec((1, CHUNK, SSM_WIDTH), lambda b, c: (b, c, 0)),
            pl.BlockSpec((1, CHUNK, CONV_CH), lambda b, c: (b, c, 0)),
        ],
        out_shape=[
            jax.ShapeDtypeStruct((bsz, seq, SSM_WIDTH), f32),
            jax.ShapeDtypeStruct((bsz, seq, CONV_CH), bf16),
        ],
        scratch_shapes=[pltpu.VMEM((N_SSM_GROUPS, D_STATE, GROUP_W), f32)],
        compiler_params=_cparams(2),
        name="ssd_fwd",
    )(xz3, xz3, xz3, dt3, conv_w, conv_b, dt_bias, a_log, d_skip, tril, triu, ef, shift)


def _ssd_bwd(act, dt3, y_part, xz3, dt_bias, a_log, norm_w, consts):
    bsz, seq = xz3.shape[0], xz3.shape[1]
    nc = seq // CHUNK
    tril, _, _, eb, _ = consts
    const = _const_spec
    rev = lambda b, c: (b, nc - 1 - c, 0)
    return pl.pallas_call(
        _ssd_bwd_kernel,
        grid=(bsz, nc),
        in_specs=[
            pl.BlockSpec((1, CHUNK, CONV_CH), rev),
            pl.BlockSpec((1, CHUNK, LANES), rev),
            pl.BlockSpec((1, CHUNK, SSM_WIDTH), rev),
            pl.BlockSpec((1, CHUNK, SSM_WIDTH), lambda b, c: (b, nc - 1 - c, CONV_CH // SSM_WIDTH)),
            const((1, LANES)), const((1, LANES)), const((1, SSM_WIDTH)),
            const((CHUNK, CHUNK)), const((LANES, SSM_WIDTH)),
        ],
        out_specs=pl.BlockSpec((1, CHUNK, SSM_WIDTH), rev),
        out_shape=jax.ShapeDtypeStruct((bsz, seq, SSM_WIDTH), bf16),
        scratch_shapes=[pltpu.VMEM((N_SSM_GROUPS, D_STATE, GROUP_W), f32)],
        compiler_params=_cparams(2),
        name="ssd_bwd",
    )(act, dt3, y_part, xz3, dt_bias, a_log, norm_w, tril, eb)


def _ssd(xz3, dt3, conv_w, conv_b, dt_bias, a_log, d_skip, norm_w, consts):
    y_part, act = _ssd_fwd(xz3, dt3, conv_w, conv_b, dt_bias, a_log, d_skip, consts)
    return _ssd_bwd(act, dt3, y_part, xz3, dt_bias, a_log, norm_w, consts)


def _ssd_consts():
    idx = np.arange(CHUNK)
    tril = (idx[None, :] <= idx[:, None]).astype(np.float32)
    head = np.arange(SSM_WIDTH) // SSM_HEAD_DIM
    rows = np.arange(LANES)
    piece = lambda first: (rows >= first) & (rows < first + 3 * N_SSM_HEADS)
    ef = (piece(0)[:, None] & (rows[:, None] % N_SSM_HEADS == head[None, :])).astype(np.float32)
    eb = (piece(N_SSM_HEADS)[:, None] & (rows[:, None] % N_SSM_HEADS == head[None, :])).astype(np.float32)
    out_row = np.arange((D_CONV - 1) * CHUNK)
    tap = np.asarray(_SIDE_TAPS)[out_row // CHUNK]
    src = HALO + out_row % CHUNK + tap - D_CONV // 2
    shift = (src[:, None] == np.arange(CHUNK + 2 * HALO)[None, :]).astype(np.float32)
    return tuple(jnp.asarray(m, bf16) for m in (tril, tril.T, ef, eb, shift))


def _out_proj_kernel(a_ref, s_ref, w_ref, x_ref, g_ref, o_ref):
    half = x_ref.shape[0] // 2
    for rows in (slice(0, half), slice(half, 2 * half)):
        mix = jnp.dot(a_ref[rows], w_ref[:ATTN_WIDTH, :], preferred_element_type=f32)
        mix = mix + jnp.dot(s_ref[rows], w_ref[ATTN_WIDTH:, :], preferred_element_type=f32)
        o_ref[rows] = x_ref[rows] + _rms(mix, g_ref[...])


def _out_proj(a2, s2, w_out, x2, gain, tm):
    rows = x2.shape[0]
    return pl.pallas_call(
        _out_proj_kernel,
        grid=(rows // tm,),
        in_specs=[
            pl.BlockSpec((tm, ATTN_WIDTH), lambda i: (i, 0)),
            pl.BlockSpec((tm, SSM_WIDTH), lambda i: (i, 0)),
            pl.BlockSpec((D_MODEL, D_MODEL), lambda i: (0, 0)),
            pl.BlockSpec((tm, D_MODEL), lambda i: (i, 0)),
            pl.BlockSpec((1, D_MODEL), lambda i: (0, 0)),
        ],
        out_specs=pl.BlockSpec((tm, D_MODEL), lambda i: (i, 0)),
        out_shape=jax.ShapeDtypeStruct((rows, D_MODEL), f32),
        compiler_params=_cparams(1),
        name="out_proj",
    )(a2, s2, w_out, x2, gain)


def _mlp_kernel(x_ref, gpre_ref, wup_ref, wdn_ref, gpost_ref, o_ref, h_ref):
    j = pl.program_id(1)
    last = pl.num_programs(1) - 1
    part = x_ref.shape[0] // MLP_EDGE_PARTS
    halves = tuple(slice(u * part, (u + 1) * part) for u in range(MLP_EDGE_PARTS))

    def ff(h):
        u = jnp.dot(h, wup_ref[...], preferred_element_type=f32)
        u = jnp.square(jnp.maximum(u, 0.0)).astype(bf16)
        return jnp.dot(u, wdn_ref[...], preferred_element_type=f32)

    @pl.when(j == 0)
    def _():
        for rows in halves:
            h = _rms(x_ref[rows], gpre_ref[...]).astype(bf16)
            h_ref[rows] = h
            o_ref[rows] = ff(h)

    @pl.when((j > 0) & (j < last))
    def _():
        o_ref[...] += ff(h_ref[...])

    @pl.when(j == last)
    def _():
        for rows in halves:
            y = o_ref[rows] + ff(h_ref[rows])
            o_ref[rows] = x_ref[rows] + _rms(y, gpost_ref[...])


def _mlp(x2, g_pre, w_up, w_down, g_post, tm, tf):
    rows = x2.shape[0]
    assert D_FF // tf >= 2, "the first and the last d_ff step are distinct code paths"
    return pl.pallas_call(
        _mlp_kernel,
        grid=(rows // tm, D_FF // tf),
        in_specs=[
            pl.BlockSpec((tm, D_MODEL), lambda i, j: (i, 0)),
            pl.BlockSpec((1, D_MODEL), lambda i, j: (0, 0)),
            pl.BlockSpec((D_MODEL, tf), lambda i, j: (0, j)),
            pl.BlockSpec((tf, D_MODEL), lambda i, j: (j, 0)),
            pl.BlockSpec((1, D_MODEL), lambda i, j: (0, 0)),
        ],
        out_specs=pl.BlockSpec((tm, D_MODEL), lambda i, j: (i, 0)),
        out_shape=jax.ShapeDtypeStruct((rows, D_MODEL), f32),
        scratch_shapes=[pltpu.VMEM((tm, D_MODEL), bf16)],
        compiler_params=_cparams(2),
        name="mlp",
    )(x2, g_pre, w_up, w_down, g_post)


def _attn_tiles(seq):
    for tq, tk in ((1024, 512), (512, 512), (256, 256)):
        if seq % tq == 0 and seq // tk >= tq // tk + 4:
            return tq, tk
    raise ValueError(f"sequence length {seq} is too short for the attention tiling")


def _trunk(x, layers, rel_bias, bias, consts, tq, tk):
    bsz, seq = x.shape[0], x.shape[1]
    rows = bsz * seq
    tm = tk
    tm_wide = ROW_TILE if seq % ROW_TILE == 0 and ROW_TILE % tk == 0 else tm
    x2 = x.reshape(rows, D_MODEL)
    for lp in layers:
        qk, vt, xz, dt = _in_proj(x2, lp["pre_norm_mix"], lp["w_main"], lp["w_dt"], bsz, seq, tm_wide, tk)
        a_out = _attention(qk.reshape(bsz, seq, 2 * ATTN_WIDTH), vt, bias, rel_bias, lp["lam"], lp["attn_norm"],
                           lp["lambda_init"], tq, tk)
        s_out = _ssd(xz.reshape(bsz, seq, CONV_CH + SSM_WIDTH), dt.reshape(bsz, seq, LANES),
                     lp["conv_w"], lp["conv_b"], lp["dt_bias"], lp["a_log"], lp["d_skip"], lp["ssm_norm"],
                     consts)
        x2 = _out_proj(a_out.reshape(rows, ATTN_WIDTH), s_out.reshape(rows, SSM_WIDTH), lp["w_out"], x2,
                       lp["post_norm_mix"], tm)
        x2 = _mlp(x2, lp["pre_norm_mlp"], lp["w_up"], lp["w_down"], lp["post_norm_mlp"], tm_wide, FF_TILE)
    return x2.reshape(bsz, seq, D_MODEL)


def _pad_lanes(v, width=LANES):
    return jnp.pad(v, (0, width - v.shape[0]))[None, :].astype(f32)


def kernel(x_prompt, x_sample, rel_bias, pre_norm_mix, w_in, lambda_q1, lambda_k1, lambda_q2, lambda_k2,
           attn_norm, conv_w, conv_b, dt_bias_fwd, dt_bias_bwd, a_log_fwd, a_log_bwd, d_skip, ssm_norm,
           w_out, post_norm_mix, pre_norm_mlp, w_up, w_down, post_norm_mlp):
    depth = w_in.shape[0]
    row = lambda v: v[None, :].astype(f32)
    layers = []
    for i in range(depth):
        lambda_init = 0.8 - 0.6 * math.exp(-0.3 * i)
        lam = (jnp.exp(jnp.sum(lambda_q1[i].astype(f32) * lambda_k1[i].astype(f32)))
               - jnp.exp(jnp.sum(lambda_q2[i].astype(f32) * lambda_k2[i].astype(f32))) + lambda_init)
        wi = w_in[i]
        w_main = wi.astype(bf16)
        w_dt = jnp.pad(wi[:, OFF_DT:], ((0, 0), (0, LANES - 2 * N_SSM_HEADS))).astype(bf16)
        layers.append(dict(
            lambda_init=lambda_init,
            lam=lam.reshape(1).astype(f32),
            pre_norm_mix=row(pre_norm_mix[i]), w_main=w_main, w_dt=w_dt,
            attn_norm=row(attn_norm[i]),
            conv_w=jnp.pad(conv_w[i].astype(f32), ((0, SUBLANES - D_CONV), (0, 0))), conv_b=row(conv_b[i]),
            dt_bias=_pad_lanes(jnp.concatenate([dt_bias_fwd[i], dt_bias_bwd[i]])),
            a_log=_pad_lanes(jnp.concatenate([a_log_fwd[i], a_log_bwd[i]])),
            d_skip=row(jnp.repeat(d_skip[i], SSM_HEAD_DIM)), ssm_norm=row(ssm_norm[i]),
            w_out=w_out[i].astype(bf16), post_norm_mix=row(post_norm_mix[i]),
            pre_norm_mlp=row(pre_norm_mlp[i]), w_up=w_up[i].astype(bf16), w_down=w_down[i].astype(bf16),
            post_norm_mlp=row(post_norm_mlp[i]),
        ))
    consts = _ssd_consts()
    rel = rel_bias.astype(f32)
    outs = []
    biases = {}
    for x in (x_prompt, x_sample):
        tq, tk = _attn_tiles(x.shape[1])
        if (tq, tk) not in biases:
            biases[(tq, tk)] = _bias_tiles(rel, tq, tk)
        outs.append(_trunk(x, layers, rel, biases[(tq, tk)], consts, tq, tk))
    return tuple(outs)
```

```python
import functools
import math

import jax
import jax.numpy as jnp
import numpy as np
from jax import lax
from jax.experimental import pallas as pl
from jax.experimental.pallas import tpu as pltpu

f32 = jnp.float32
bf16 = jnp.bfloat16

D_MODEL = 2048
N_ATTN_HEADS = 8
ATTN_HEAD_DIM = 64
ATTN_V_DIM = 2 * ATTN_HEAD_DIM
ATTN_WIDTH = N_ATTN_HEADS * ATTN_V_DIM
SSM_WIDTH = 1024
SSM_HEAD_DIM = 64
N_SSM_HEADS = 16
N_SSM_GROUPS = 4
SSM_HPG = 4
D_STATE = 128
D_CONV = 5
CHUNK = 128
CONV_CH = SSM_WIDTH + 2 * N_SSM_GROUPS * D_STATE
GROUP_W = SSM_HPG * SSM_HEAD_DIM
OFF_Q, OFF_K, OFF_V, OFF_Z = 0, 1024, 2048, 3072
OFF_XBC = 4096
OFF_DT = OFF_XBC + CONV_CH
D_FF = 4 * D_MODEL
NUM_BUCKETS = 32
MAX_DISTANCE = 128
EPS = 1e-6
NEG = -1e30
LOG2E = math.log2(math.e)

LANES = 128
SUBLANES = 8
ROW_TILE = 1024
FF_TILE = 512
MLP_EDGE_PARTS = 2
TILE_UNROLL = 2
HALO = 16
_SIDE_TAPS = tuple(j for j in range(D_CONV) if j != D_CONV // 2)
VMEM_LIMIT = 56 * 1024 * 1024


def _cparams(n_axes):
    return pltpu.CompilerParams(dimension_semantics=("arbitrary",) * n_axes,
                                vmem_limit_bytes=VMEM_LIMIT)


def _rms(xf, g):
    ms = jnp.mean(xf * xf, axis=-1, keepdims=True)
    return (xf * lax.rsqrt(ms + EPS)) * g


def _split3(v):
    hi = v.astype(bf16)
    r1 = v - hi.astype(f32)
    mid = r1.astype(bf16)
    lo = (r1 - mid.astype(f32)).astype(bf16)
    return hi, mid, lo


def _dot3(v, m01):
    hi, mid, lo = _split3(v)
    d = lambda a: jnp.dot(a, m01, preferred_element_type=f32)
    return d(hi) + d(mid) + d(lo)


def _expand_heads(v, e3, first):
    lane = lax.broadcasted_iota(jnp.int32, v.shape, 1)
    hi, mid, lo = _split3(jnp.where((lane >= first) & (lane < first + N_SSM_HEADS), v, 0.0))
    packed = (hi.astype(f32) + pltpu.roll(mid.astype(f32), N_SSM_HEADS, 1)
              + pltpu.roll(lo.astype(f32), 2 * N_SSM_HEADS, 1))
    return jnp.dot(packed.astype(bf16), e3, preferred_element_type=f32)


def _dot3_left(m01, v):
    hi, mid, lo = _split3(v)
    d = lambda a: jnp.dot(m01, a, preferred_element_type=f32)
    return d(hi) + d(mid) + d(lo)


def _in_proj_kernel(x_ref, g_ref, w_ref, wdt_ref, qk_ref, vt_ref, xz_ref, dt_ref, h_ref):
    j = pl.program_id(1)
    half = x_ref.shape[0] // 2

    @pl.when(j == 0)
    def _():
        for rows in (slice(0, half), slice(half, 2 * half)):
            hb = _rms(x_ref[rows], g_ref[...]).astype(bf16)
            h_ref[rows] = hb
            dt_ref[rows] = jnp.dot(hb, wdt_ref[...], preferred_element_type=f32)
            q = jnp.dot(hb, w_ref[...], preferred_element_type=f32)
            qk_ref[rows] = (q * (ATTN_HEAD_DIM ** -0.5 * LOG2E)).astype(bf16)

    @pl.when(j == 2)
    def _():
        tk = vt_ref.shape[-1]
        for u in range(vt_ref.shape[2]):
            v = jnp.dot(h_ref[u * tk:(u + 1) * tk], w_ref[...], preferred_element_type=f32)
            vt_ref[0, :, u] = v.T.astype(bf16).reshape(N_ATTN_HEADS, ATTN_V_DIM, tk)

    def project(out_ref):
        for rows in (slice(0, half), slice(half, 2 * half)):
            out_ref[rows] = jnp.dot(h_ref[rows], w_ref[...], preferred_element_type=f32).astype(bf16)

    @pl.when(j == 1)
    def _():
        project(qk_ref)

    @pl.when(j >= 3)
    def _():
        project(xz_ref)


def _in_proj(x2, gain, w_main, w_dt, bsz, seq, tm, tk):
    rows = x2.shape[0]
    nl = seq // tm
    nj = OFF_DT // ATTN_WIDTH
    n_qkv, z_tile = OFF_Z // ATTN_WIDTH, OFF_Z // ATTN_WIDTH
    w_tile = lambda i, j: (0, jnp.where(j < n_qkv, j, jnp.where(j < nj - 1, j + 1, z_tile)))
    return pl.pallas_call(
        _in_proj_kernel,
        grid=(rows // tm, nj),
        in_specs=[
            pl.BlockSpec((tm, D_MODEL), lambda i, j: (i, 0)),
            pl.BlockSpec((1, D_MODEL), lambda i, j: (0, 0)),
            pl.BlockSpec((D_MODEL, ATTN_WIDTH), w_tile),
            pl.BlockSpec((D_MODEL, LANES), lambda i, j: (0, 0)),
        ],
        out_specs=[
            pl.BlockSpec((tm, ATTN_WIDTH), lambda i, j: (i, jnp.minimum(j, 1))),
            pl.BlockSpec((1, N_ATTN_HEADS, tm // tk, ATTN_V_DIM, tk), lambda i, j: (i // nl, 0, i % nl, 0, 0)),
            pl.BlockSpec((tm, ATTN_WIDTH), lambda i, j: (i, jnp.clip(j - 3, 0, 2))),
            pl.BlockSpec((tm, LANES), lambda i, j: (i, 0)),
        ],
        out_shape=[
            jax.ShapeDtypeStruct((rows, 2 * ATTN_WIDTH), bf16),
            jax.ShapeDtypeStruct((bsz, N_ATTN_HEADS, seq // tk, ATTN_V_DIM, tk), bf16),
            jax.ShapeDtypeStruct((rows, CONV_CH + SSM_WIDTH), bf16),
            jax.ShapeDtypeStruct((rows, LANES), f32),
        ],
        scratch_shapes=[pltpu.VMEM((tm, D_MODEL), bf16)],
        compiler_params=_cparams(2),
        name="in_proj",
    )(x2, gain, w_main, w_dt)


def _bias_block(tab_ref, h, delta):
    kk = lax.broadcasted_iota(jnp.int32, (MAX_DISTANCE, MAX_DISTANCE), 0)
    qq = lax.broadcasted_iota(jnp.int32, (MAX_DISTANCE, MAX_DISTANCE), 1)
    rel = delta * MAX_DISTANCE + kk - qq
    nb = NUM_BUCKETS // 2
    max_exact = nb // 2
    ret = jnp.where(rel > 0, nb, 0)
    n = jnp.abs(rel)
    nf = jnp.maximum(n, 1).astype(f32)
    large = max_exact + (jnp.log(nf / max_exact) / math.log(MAX_DISTANCE / max_exact)
                         * (nb - max_exact)).astype(jnp.int32)
    large = jnp.minimum(large, nb - 1)
    bucket = ret + jnp.where(n < max_exact, n, large)
    out = jnp.zeros(rel.shape, f32)
    for b in range(NUM_BUCKETS):
        out = jnp.where(bucket == b, tab_ref[b, h] * LOG2E, out)
    return out


def _bias_kernel(tab_ref, o_ref, *, tq, tk):
    h = pl.program_id(0)
    blk = MAX_DISTANCE
    side = {-2: jnp.full((blk, blk), tab_ref[NUM_BUCKETS // 2 - 1, h] * LOG2E, f32),
            2: jnp.full((blk, blk), tab_ref[NUM_BUCKETS - 1, h] * LOG2E, f32)}
    band = {delta: _bias_block(tab_ref, h, delta) for delta in (-1, 0, 1)}
    for d in range(tq // tk + 2):
        for a in range(tk // blk):
            for b in range(tq // blk):
                delta = (d - 1) * (tk // blk) + a - b
                val = band[delta] if abs(delta) <= 1 else side[2 if delta > 0 else -2]
                o_ref[0, d, a * blk:(a + 1) * blk, b * blk:(b + 1) * blk] = val


def _bias_tiles(rel_bias, tq, tk):
    n_near = tq // tk + 2
    return pl.pallas_call(
        functools.partial(_bias_kernel, tq=tq, tk=tk),
        grid=(N_ATTN_HEADS,),
        in_specs=[pl.BlockSpec(memory_space=pltpu.SMEM)],
        out_specs=pl.BlockSpec((1, n_near, tk, tq), lambda h: (h, 0, 0, 0)),
        out_shape=jax.ShapeDtypeStruct((N_ATTN_HEADS, n_near, tk, tq), f32),
        compiler_params=_cparams(1),
        name="bias_tiles",
    )(rel_bias)


def _attn_kernel(tab_ref, lam_ref, q_ref, k_ref, vt_ref, bias_ref, w_ref, o_ref,
                 qcat_ref, s_ref, p_ref, m_ref, acc_ref, *, tq, tk, nk, out_scale):
    h = pl.program_id(1)
    qi = pl.program_id(2)
    r = tq // tk
    c_left = tab_ref[NUM_BUCKETS // 2 - 1, h] * LOG2E
    c_right = tab_ref[NUM_BUCKETS - 1, h] * LOG2E
    lam = lam_ref[0]

    qt = q_ref[0].astype(f32).T
    row = lax.broadcasted_iota(jnp.int32, qt.shape, 0)
    qcat_ref[...] = jnp.concatenate([jnp.where(row < ATTN_HEAD_DIM, qt, 0.0),
                                     jnp.where(row >= ATTN_HEAD_DIM, qt, 0.0)], axis=1).astype(bf16)
    ones = jnp.ones((SUBLANES, tk), bf16)
    near_lo = jnp.maximum(qi * r - 1, 0)
    near_hi = jnp.minimum(qi * r + r + 1, nk)
    n_far = nk - (near_hi - near_lo)

    def logits(kt):
        return jnp.dot(k_ref[0, kt], qcat_ref[...], preferred_element_type=f32)

    def pv(kt, p):
        vext = jnp.concatenate([vt_ref[0, 0, kt], ones], axis=0)
        return jnp.dot(vext, p, preferred_element_type=f32)

    def tile_kt(t):
        far = t + jnp.where(t >= near_lo, near_hi - near_lo, 0)
        return jnp.where(t < n_far, far, near_lo + t - n_far)

    def tile_c(t):
        return jnp.where(t < n_far, jnp.where(t >= near_lo, c_right, c_left), 0.0)

    def stage_a(t, near):
        kt = tile_kt(t)
        s = logits(kt)
        if near:
            b = bias_ref[0, kt - qi * r + 1]
            s = s + jnp.concatenate([b, b], axis=1)
        s_ref[...] = s
        return jnp.max(s, axis=0, keepdims=True)

    def stage_b(t, smax):
        c = tile_c(t)
        m = m_ref[...]
        m_new = jnp.maximum(m, smax + c)
        p_ref[...] = jnp.exp2(s_ref[...] - (m_new - c)).astype(bf16)
        m_ref[...] = m_new
        return jnp.exp2(m - m_new)

    def stage_c(t, alpha):
        acc_ref[...] = acc_ref[...] * alpha + pv(tile_kt(t), p_ref[...])

    m_ref[...] = jnp.full(m_ref.shape, NEG, f32)
    acc_ref[...] = jnp.zeros(acc_ref.shape, f32)

    smax = stage_a(0, False)
    alpha = stage_b(0, smax)
    smax = stage_a(1, False)

    def step(t, carry, near):
        alpha, smax = carry
        stage_c(t - 2, alpha)
        alpha = stage_b(t - 1, smax)
        return alpha, stage_a(t, near)

    def run_steps(lo, hi, near, carry):
        first = lo + jnp.bitwise_and(hi - lo, TILE_UNROLL - 1)
        carry = lax.fori_loop(lo, first, lambda t, cr: step(t, cr, near), carry)

        def group(j, cr):
            for u in range(TILE_UNROLL):
                cr = step(first + TILE_UNROLL * j + u, cr, near)
            return cr

        n_groups = lax.shift_right_logical(hi - first, TILE_UNROLL.bit_length() - 1)
        return lax.fori_loop(0, n_groups, group, carry)

    carry = run_steps(2, n_far, False, (alpha, smax))
    alpha, smax = run_steps(n_far, nk, True, carry)
    stage_c(nk - 2, alpha)
    alpha = stage_b(nk - 1, smax)
    stage_c(nk - 1, alpha)

    acc = acc_ref[...]
    o = acc[:ATTN_V_DIM] * (1.0 / acc[ATTN_V_DIM:ATTN_V_DIM + 1])
    diff = (o[:, :tq] - lam * o[:, tq:]).T
    o_ref[0] = (_rms(diff, w_ref[...]) * out_scale).astype(bf16)


def _attention(qk3, vt, bias, rel_bias, lam, subln_w, lambda_init, tq, tk):
    bsz, seq = qk3.shape[0], qk3.shape[1]
    nk = seq // tk
    assert nk >= tq // tk + 4, "the far-tile pipeline needs at least two far key tiles per query tile"
    qk4 = qk3.reshape(bsz, nk, tk, 2 * ATTN_WIDTH)
    kern = functools.partial(_attn_kernel, tq=tq, tk=tk, nk=nk, out_scale=1.0 - lambda_init)
    return pl.pallas_call(
        kern,
        grid=(bsz, N_ATTN_HEADS, seq // tq),
        in_specs=[
            pl.BlockSpec(memory_space=pltpu.SMEM),
            pl.BlockSpec(memory_space=pltpu.SMEM),
            pl.BlockSpec((1, tq, ATTN_V_DIM), lambda b, h, i: (b, i, h)),
            pl.BlockSpec((1, nk, tk, ATTN_V_DIM), lambda b, h, i: (b, 0, 0, N_ATTN_HEADS + h)),
            pl.BlockSpec((1, 1, nk, ATTN_V_DIM, tk), lambda b, h, i: (b, h, 0, 0, 0)),
            pl.BlockSpec((1, tq // tk + 2, tk, tq), lambda b, h, i: (h, 0, 0, 0)),
            pl.BlockSpec((1, ATTN_V_DIM), lambda b, h, i: (0, 0)),
        ],
        out_specs=pl.BlockSpec((1, tq, ATTN_V_DIM), lambda b, h, i: (b, i, h)),
        out_shape=jax.ShapeDtypeStruct((bsz, seq, ATTN_WIDTH), bf16),
        scratch_shapes=[
            pltpu.VMEM((ATTN_V_DIM, 2 * tq), bf16),
            pltpu.VMEM((tk, 2 * tq), f32),
            pltpu.VMEM((tk, 2 * tq), bf16),
            pltpu.VMEM((1, 2 * tq), f32),
            pltpu.VMEM((ATTN_V_DIM + SUBLANES, 2 * tq), f32),
        ],
        compiler_params=_cparams(3),
        name="attention",
    )(rel_bias, lam, qk3, qk4, vt, bias, subln_w)


def _softplus(v):
    return jnp.maximum(v, 0.0) + jnp.log1p(jnp.exp(-jnp.abs(v)))


def _ssd_decay_terms(dt_ref, dtb_ref, alog_ref, tril_ref):
    dtv = _softplus(dt_ref[0] + dtb_ref[...])
    lane = lax.broadcasted_iota(jnp.int32, (1, LANES), 1)
    a_row = jnp.where(lane < 2 * N_SSM_HEADS, -jnp.exp(alog_ref[...]), 0.0)
    a = dtv * a_row
    cs = _dot3_left(tril_ref[...], a)
    return dtv, a, cs


def _ssd_fwd_kernel(cur_ref, prev_ref, next_ref, dt_ref, cw_ref, cb_ref, dtb_ref, alog_ref, dskip_ref,
                    tril_ref, triu_ref, ef_ref, shift_ref, y_ref, act_ref, state_ref):
    c = pl.program_id(1)
    nc = pl.num_programs(1)

    @pl.when(c == 0)
    def _():
        state_ref[...] = jnp.zeros_like(state_ref)

    pmask = (c > 0).astype(f32)
    nmask = (c < nc - 1).astype(f32)
    ext = jnp.concatenate([prev_ref[0] * pmask.astype(bf16), cur_ref[0], next_ref[0] * nmask.astype(bf16)],
                          axis=0)
    taps = jnp.dot(shift_ref[...], ext, preferred_element_type=f32)
    mid = D_CONV // 2
    conv = cur_ref[0].astype(f32) * cw_ref[mid:mid + 1, :] + cb_ref[...]
    for i, j in enumerate(_SIDE_TAPS):
        conv = conv + taps[i * CHUNK:(i + 1) * CHUNK, :] * cw_ref[j:j + 1, :]
    act = conv * (1.0 / (1.0 + jnp.exp(-conv)))
    actb = act.astype(bf16)
    act_ref[0] = actb

    dtv, a, cs = _ssd_decay_terms(dt_ref, dtb_ref, alog_ref, tril_ref)
    a_t = a.T
    dt_t = dtv.T
    cs_t = _dot3(a_t, triu_ref[...])
    ecs = cs - a
    ecs_t = cs_t - a_t

    tot = cs[CHUNK - 1:CHUNK, :]
    scale_off = _expand_heads(jnp.exp(cs), ef_ref[...], 0)
    scale_in = _expand_heads(dtv * jnp.exp(tot - cs), ef_ref[...], 0)
    carry_dec = scale_off[CHUNK - 1:CHUNK, :]

    log2_dt_t = jnp.log(dt_t) * LOG2E
    col_f, row_f = cs * LOG2E, cs_t * LOG2E - log2_dt_t
    col_b, row_b = ecs * LOG2E, ecs_t * LOG2E + log2_dt_t
    li = lax.broadcasted_iota(jnp.int32, (CHUNK, CHUNK), 0)
    si = lax.broadcasted_iota(jnp.int32, (CHUNK, CHUNK), 1)
    lower = li >= si
    upper = si >= li

    for g in range(N_SSM_GROUPS):
        gs = slice(g * GROUP_W, (g + 1) * GROUP_W)
        bg = actb[:, SSM_WIDTH + g * D_STATE: SSM_WIDTH + (g + 1) * D_STATE]
        cg = actb[:, SSM_WIDTH + N_SSM_GROUPS * D_STATE + g * D_STATE:
                  SSM_WIDTH + N_SSM_GROUPS * D_STATE + (g + 1) * D_STATE]
        cbm = lax.dot_general(cg, bg, (((1,), (1,)), ((), ())), preferred_element_type=f32)
        ws, xs_blocks = [], []
        for hh in range(SSM_HPG):
            hd = g * SSM_HPG + hh
            hb = N_SSM_HEADS + hd
            dec_f = jnp.exp2(jnp.where(lower, col_f[:, hd:hd + 1] - row_f[hd:hd + 1, :], NEG))
            dec_b = jnp.exp2(jnp.where(upper, row_b[hb:hb + 1, :] - col_b[:, hb:hb + 1], NEG))
            ws.append((cbm * (dec_f + dec_b)).astype(bf16))
            lane = lax.broadcasted_iota(jnp.int32, (CHUNK, GROUP_W), 1)
            own = (lane >= hh * SSM_HEAD_DIM) & (lane < (hh + 1) * SSM_HEAD_DIM)
            xs_blocks.append(jnp.where(own, actb[:, gs], jnp.zeros((), bf16)))
        xg = act[:, gs]
        wcat = jnp.concatenate(ws, axis=1)
        xblk = jnp.concatenate(xs_blocks, axis=0)
        y = jnp.dot(wcat, xblk, preferred_element_type=f32)
        y = y + jnp.dot(cg, state_ref[g].astype(bf16), preferred_element_type=f32) * scale_off[:, gs]
        y_ref[0, :, gs] = y + xg * dskip_ref[:, gs]
        xs = (xg * scale_in[:, gs]).astype(bf16)
        new = jnp.dot(bg.astype(f32).T.astype(bf16), xs, preferred_element_type=f32)
        state_ref[g] = state_ref[g] * carry_dec[:, gs] + new


def _ssd_bwd_kernel(act_ref, dt_ref, y_ref, z_ref, dtb_ref, alog_ref, nw_ref, tril_ref, eb_ref,
                    o_ref, state_ref):
    c = pl.program_id(1)

    @pl.when(c == 0)
    def _():
        state_ref[...] = jnp.zeros_like(state_ref)

    actb = act_ref[0]
    dtv, a, cs = _ssd_decay_terms(dt_ref, dtb_ref, alog_ref, tril_ref)
    ecs = cs - a
    tot = cs[CHUNK - 1:CHUNK, :]
    scale_off = _expand_heads(jnp.exp(tot - ecs), eb_ref[...], N_SSM_HEADS)
    scale_in = _expand_heads(dtv * jnp.exp(ecs), eb_ref[...], N_SSM_HEADS)
    carry_dec = scale_off[0:1, :]

    zf = z_ref[0].astype(f32)
    gate = zf * (1.0 / (1.0 + jnp.exp(-zf)))
    for g in range(N_SSM_GROUPS):
        gs = slice(g * GROUP_W, (g + 1) * GROUP_W)
        bg = actb[:, SSM_WIDTH + g * D_STATE: SSM_WIDTH + (g + 1) * D_STATE]
        cg = actb[:, SSM_WIDTH + N_SSM_GROUPS * D_STATE + g * D_STATE:
                  SSM_WIDTH + N_SSM_GROUPS * D_STATE + (g + 1) * D_STATE]
        y = y_ref[0, :, gs] + (jnp.dot(cg, state_ref[g].astype(bf16), preferred_element_type=f32)
                               * scale_off[:, gs])
        xs = (actb[:, gs].astype(f32) * scale_in[:, gs]).astype(bf16)
        new = jnp.dot(bg.astype(f32).T.astype(bf16), xs, preferred_element_type=f32)
        state_ref[g] = state_ref[g] * carry_dec[:, gs] + new
        o_ref[0, :, gs] = _rms(y * gate[:, gs], nw_ref[:, gs]).astype(bf16)


def _const_spec(shape):
    return pl.BlockSpec(shape, lambda b, c: (0,) * len(shape))


def _ssd_fwd(xz3, dt3, conv_w, conv_b, dt_bias, a_log, d_skip, consts):
    bsz, seq = xz3.shape[0], xz3.shape[1]
    nc = seq // CHUNK
    hb = CHUNK // HALO
    nh = seq // HALO
    tril, triu, ef, _, shift = consts
    const = _const_spec
    return pl.pallas_call(
        _ssd_fwd_kernel,
        grid=(bsz, nc),
        in_specs=[
            pl.BlockSpec((1, CHUNK, CONV_CH), lambda b, c: (b, c, 0)),
            pl.BlockSpec((1, HALO, CONV_CH), lambda b, c: (b, jnp.maximum(c * hb - 1, 0), 0)),
            pl.BlockSpec((1, HALO, CONV_CH), lambda b, c: (b, jnp.minimum((c + 1) * hb, nh - 1), 0)),
            pl.BlockSpec((1, CHUNK, LANES), lambda b, c: (b, c, 0)),
            const((SUBLANES, CONV_CH)), const((1, CONV_CH)), const((1, LANES)), const((1, LANES)),
            const((1, SSM_WIDTH)), const((CHUNK, CHUNK)), const((CHUNK, CHUNK)), const((LANES, SSM_WIDTH)),
            const(((D_CONV - 1) * CHUNK, CHUNK + 2 * HALO)),
        ],
        out_specs=[
            pl.BlockSpec((1, CHUNK, SSM_WIDTH), lambda b, c: (b, c, 0)),
            pl.BlockSpec((1, CHUNK, CONV_CH), lambda b, c: (b, c, 0)),
        ],
        out_shape=[
            jax.ShapeDtypeStruct((bsz, seq, SSM_WIDTH), f32),
            jax.ShapeDtypeStruct((bsz, seq, CONV_CH), bf16),
        ],
        scratch_shapes=[pltpu.VMEM((N_SSM_GROUPS, D_STATE, GROUP_W), f32)],
        compiler_params=_cparams(2),
        name="ssd_fwd",
    )(xz3, xz3, xz3, dt3, conv_w, conv_b, dt_bias, a_log, d_skip, tril, triu, ef, shift)


def _ssd_bwd(act, dt3, y_part, xz3, dt_bias, a_log, norm_w, consts):
    bsz, seq = xz3.shape[0], xz3.shape[1]
    nc = seq // CHUNK
    tril, _, _, eb, _ = consts
    const = _const_spec
    rev = lambda b, c: (b, nc - 1 - c, 0)
    return pl.pallas_call(
        _ssd_bwd_kernel,
        grid=(bsz, nc),
        in_specs=[
            pl.BlockSpec((1, CHUNK, CONV_CH), rev),
            pl.BlockSpec((1, CHUNK, LANES), rev),
            pl.BlockSpec((1, CHUNK, SSM_WIDTH), rev),
            pl.BlockSpec((1, CHUNK, SSM_WIDTH), lambda b, c: (b, nc - 1 - c, CONV_CH // SSM_WIDTH)),
            const((1, LANES)), const((1, LANES)), const((1, SSM_WIDTH)),
            const((CHUNK, CHUNK)), const((LANES, SSM_WIDTH)),
        ],
        out_specs=pl.BlockSpec((1, CHUNK, SSM_WIDTH), rev),
        out_shape=jax.ShapeDtypeStruct((bsz, seq, SSM_WIDTH), bf16),
        scratch_shapes=[pltpu.VMEM((N_SSM_GROUPS, D_STATE, GROUP_W), f32)],
        compiler_params=_cparams(2),
        name="ssd_bwd",
    )(act, dt3, y_part, xz3, dt_bias, a_log, norm_w, tril, eb)


def _ssd(xz3, dt3, conv_w, conv_b, dt_bias, a_log, d_skip, norm_w, consts):
    y_part, act = _ssd_fwd(xz3, dt3, conv_w, conv_b, dt_bias, a_log, d_skip, consts)
    return _ssd_bwd(act, dt3, y_part, xz3, dt_bias, a_log, norm_w, consts)


def _ssd_consts():
    idx = np.arange(CHUNK)
    tril = (idx[None, :] <= idx[:, None]).astype(np.float32)
    head = np.arange(SSM_WIDTH) // SSM_HEAD_DIM
    rows = np.arange(LANES)
    piece = lambda first: (rows >= first) & (rows < first + 3 * N_SSM_HEADS)
    ef = (piece(0)[:, None] & (rows[:, None] % N_SSM_HEADS == head[None, :])).astype(np.float32)
    eb = (piece(N_SSM_HEADS)[:, None] & (rows[:, None] % N_SSM_HEADS == head[None, :])).astype(np.float32)
    out_row = np.arange((D_CONV - 1) * CHUNK)
    tap = np.asarray(_SIDE_TAPS)[out_row // CHUNK]
    src = HALO + out_row % CHUNK + tap - D_CONV // 2
    shift = (src[:, None] == np.arange(CHUNK + 2 * HALO)[None, :]).astype(np.float32)
    return tuple(jnp.asarray(m, bf16) for m in (tril, tril.T, ef, eb, shift))


def _out_proj_kernel(a_ref, s_ref, w_ref, x_ref, g_ref, o_ref):
    mix = jnp.dot(a_ref[...], w_ref[:ATTN_WIDTH, :], preferred_element_type=f32)
    mix = mix + jnp.dot(s_ref[...], w_ref[ATTN_WIDTH:, :], preferred_element_type=f32)
    o_ref[...] = x_ref[...] + _rms(mix, g_ref[...])


def _out_proj(a2, s2, w_out, x2, gain, tm):
    rows = x2.shape[0]
    return pl.pallas_call(
        _out_proj_kernel,
        grid=(rows // tm,),
        in_specs=[
            pl.BlockSpec((tm, ATTN_WIDTH), lambda i: (i, 0)),
            pl.BlockSpec((tm, SSM_WIDTH), lambda i: (i, 0)),
            pl.BlockSpec((D_MODEL, D_MODEL), lambda i: (0, 0)),
            pl.BlockSpec((tm, D_MODEL), lambda i: (i, 0)),
            pl.BlockSpec((1, D_MODEL), lambda i: (0, 0)),
        ],
        out_specs=pl.BlockSpec((tm, D_MODEL), lambda i: (i, 0)),
        out_shape=jax.ShapeDtypeStruct((rows, D_MODEL), f32),
        compiler_params=_cparams(1),
        name="out_proj",
    )(a2, s2, w_out, x2, gain)


def _mlp_kernel(x_ref, gpre_ref, wup_ref, wdn_ref, gpost_ref, o_ref, h_ref):
    j = pl.program_id(1)
    last = pl.num_programs(1) - 1
    part = x_ref.shape[0] // MLP_EDGE_PARTS
    halves = tuple(slice(u * part, (u + 1) * part) for u in range(MLP_EDGE_PARTS))

    def ff(h):
        u = jnp.dot(h, wup_ref[...], preferred_element_type=f32)
        u = jnp.square(jnp.maximum(u, 0.0)).astype(bf16)
        return jnp.dot(u, wdn_ref[...], preferred_element_type=f32)

    @pl.when(j == 0)
    def _():
        for rows in halves:
            h = _rms(x_ref[rows], gpre_ref[...]).astype(bf16)
            h_ref[rows] = h
            o_ref[rows] = ff(h)

    @pl.when((j > 0) & (j < last))
    def _():
        o_ref[...] += ff(h_ref[...])

    @pl.when(j == last)
    def _():
        for rows in halves:
            y = o_ref[rows] + ff(h_ref[rows])
            o_ref[rows] = x_ref[rows] + _rms(y, gpost_ref[...])


def _mlp(x2, g_pre, w_up, w_down, g_post, tm, tf):
    rows = x2.shape[0]
    assert D_FF // tf >= 2, "the first and the last d_ff step are distinct code paths"
    return pl.pallas_call(
        _mlp_kernel,
        grid=(rows // tm, D_FF // tf),
        in_specs=[
            pl.BlockSpec((tm, D_MODEL), lambda i, j: (i, 0)),
            pl.BlockSpec((1, D_MODEL), lambda i, j: (0, 0)),
            pl.BlockSpec((D_MODEL, tf), lambda i, j: (0, j)),
            pl.BlockSpec((tf, D_MODEL), lambda i, j: (j, 0)),
            pl.BlockSpec((1, D_MODEL), lambda i, j: (0, 0)),
        ],
        out_specs=pl.BlockSpec((tm, D_MODEL), lambda i, j: (i, 0)),
        out_shape=jax.ShapeDtypeStruct((rows, D_MODEL), f32),
        scratch_shapes=[pltpu.VMEM((tm, D_MODEL), bf16)],
        compiler_params=_cparams(2),
        name="mlp",
    )(x2, g_pre, w_up, w_down, g_post)


def _attn_tiles(seq):
    for tq, tk in ((1024, 512), (512, 512), (256, 256)):
        if seq % tq == 0 and seq // tk >= tq // tk + 4:
            return tq, tk
    raise ValueError(f"sequence length {seq} is too short for the attention tiling")


def _trunk(x, layers, rel_bias, bias, consts, tq, tk):
    bsz, seq = x.shape[0], x.shape[1]
    rows = bsz * seq
    tm = tk
    tm_wide = ROW_TILE if seq % ROW_TILE == 0 and ROW_TILE % tk == 0 else tm
    x2 = x.reshape(rows, D_MODEL)
    for lp in layers:
        qk, vt, xz, dt = _in_proj(x2, lp["pre_norm_mix"], lp["w_main"], lp["w_dt"], bsz, seq, tm_wide, tk)
        a_out = _attention(qk.reshape(bsz, seq, 2 * ATTN_WIDTH), vt, bias, rel_bias, lp["lam"], lp["attn_norm"],
                           lp["lambda_init"], tq, tk)
        s_out = _ssd(xz.reshape(bsz, seq, CONV_CH + SSM_WIDTH), dt.reshape(bsz, seq, LANES),
                     lp["conv_w"], lp["conv_b"], lp["dt_bias"], lp["a_log"], lp["d_skip"], lp["ssm_norm"],
                     consts)
        x2 = _out_proj(a_out.reshape(rows, ATTN_WIDTH), s_out.reshape(rows, SSM_WIDTH), lp["w_out"], x2,
                       lp["post_norm_mix"], tm)
        x2 = _mlp(x2, lp["pre_norm_mlp"], lp["w_up"], lp["w_down"], lp["post_norm_mlp"], tm_wide, FF_TILE)
    return x2.reshape(bsz, seq, D_MODEL)


def _pad_lanes(v, width=LANES):
    return jnp.pad(v, (0, width - v.shape[0]))[None, :].astype(f32)


def kernel(x_prompt, x_sample, rel_bias, pre_norm_mix, w_in, lambda_q1, lambda_k1, lambda_q2, lambda_k2,
           attn_norm, conv_w, conv_b, dt_bias_fwd, dt_bias_bwd, a_log_fwd, a_log_bwd, d_skip, ssm_norm,
           w_out, post_norm_mix, pre_norm_mlp, w_up, w_down, post_norm_mlp):
    depth = w_in.shape[0]
    row = lambda v: v[None, :].astype(f32)
    layers = []
    for i in range(depth):
        lambda_init = 0.8 - 0.6 * math.exp(-0.3 * i)
        lam = (jnp.exp(jnp.sum(lambda_q1[i].astype(f32) * lambda_k1[i].astype(f32)))
               - jnp.exp(jnp.sum(lambda_q2[i].astype(f32) * lambda_k2[i].astype(f32))) + lambda_init)
        wi = w_in[i]
        w_main = wi.astype(bf16)
        w_dt = jnp.pad(wi[:, OFF_DT:], ((0, 0), (0, LANES - 2 * N_SSM_HEADS))).astype(bf16)
        layers.append(dict(
            lambda_init=lambda_init,
            lam=lam.reshape(1).astype(f32),
            pre_norm_mix=row(pre_norm_mix[i]), w_main=w_main, w_dt=w_dt,
            attn_norm=row(attn_norm[i]),
            conv_w=jnp.pad(conv_w[i].astype(f32), ((0, SUBLANES - D_CONV), (0, 0))), conv_b=row(conv_b[i]),
            dt_bias=_pad_lanes(jnp.concatenate([dt_bias_fwd[i], dt_bias_bwd[i]])),
            a_log=_pad_lanes(jnp.concatenate([a_log_fwd[i], a_log_bwd[i]])),
            d_skip=row(jnp.repeat(d_skip[i], SSM_HEAD_DIM)), ssm_norm=row(ssm_norm[i]),
            w_out=w_out[i].astype(bf16), post_norm_mix=row(post_norm_mix[i]),
            pre_norm_mlp=row(pre_norm_mlp[i]), w_up=w_up[i].astype(bf16), w_down=w_down[i].astype(bf16),
            post_norm_mlp=row(post_norm_mlp[i]),
        ))
    consts = _ssd_consts()
    rel = rel_bias.astype(f32)
    outs = []
    biases = {}
    for x in (x_prompt, x_sample):
        tq, tk = _attn_tiles(x.shape[1])
        if (tq, tk) not in biases:
            biases[(tq, tk)] = _bias_tiles(rel, tq, tk)
        outs.append(_trunk(x, layers, rel, biases[(tq, tk)], consts, tq, tk))
    return tuple(outs)
```

```python
import functools
import math

import jax
import jax.numpy as jnp
import numpy as np
from jax import lax
from jax.experimental import pallas as pl
from jax.experimental.pallas import tpu as pltpu

f32 = jnp.float32
bf16 = jnp.bfloat16

D_MODEL = 2048
N_ATTN_HEADS = 8
ATTN_HEAD_DIM = 64
ATTN_V_DIM = 2 * ATTN_HEAD_DIM
ATTN_WIDTH = N_ATTN_HEADS * ATTN_V_DIM
SSM_WIDTH = 1024
SSM_HEAD_DIM = 64
N_SSM_HEADS = 16
N_SSM_GROUPS = 4
SSM_HPG = 4
D_STATE = 128
D_CONV = 5
CHUNK = 128
CONV_CH = SSM_WIDTH + 2 * N_SSM_GROUPS * D_STATE
GROUP_W = SSM_HPG * SSM_HEAD_DIM
OFF_Q, OFF_K, OFF_V, OFF_Z = 0, 1024, 2048, 3072
OFF_XBC = 4096
OFF_DT = OFF_XBC + CONV_CH
D_FF = 4 * D_MODEL
NUM_BUCKETS = 32
MAX_DISTANCE = 128
EPS = 1e-6
NEG = -1e30
LOG2E = math.log2(math.e)

LANES = 128
SUBLANES = 8
ROW_TILE = 1024
FF_TILE = 512
MLP_EDGE_PARTS = 2
TILE_UNROLL = 2
SSD_BWD_CHUNKS = 4
HALO = 16
_SIDE_TAPS = tuple(j for j in range(D_CONV) if j != D_CONV // 2)
VMEM_LIMIT = 56 * 1024 * 1024


def _cparams(n_axes):
    return pltpu.CompilerParams(dimension_semantics=("arbitrary",) * n_axes,
                                vmem_limit_bytes=VMEM_LIMIT)


def _rms(xf, g):
    ms = jnp.mean(xf * xf, axis=-1, keepdims=True)
    return (xf * lax.rsqrt(ms + EPS)) * g


def _split3(v):
    hi = v.astype(bf16)
    r1 = v - hi.astype(f32)
    mid = r1.astype(bf16)
    lo = (r1 - mid.astype(f32)).astype(bf16)
    return hi, mid, lo


def _dot3(v, m01):
    hi, mid, lo = _split3(v)
    d = lambda a: jnp.dot(a, m01, preferred_element_type=f32)
    return d(hi) + d(mid) + d(lo)


def _expand_heads(v, e3, first):
    lane = lax.broadcasted_iota(jnp.int32, v.shape, 1)
    hi, mid, lo = _split3(jnp.where((lane >= first) & (lane < first + N_SSM_HEADS), v, 0.0))
    packed = (hi.astype(f32) + pltpu.roll(mid.astype(f32), N_SSM_HEADS, 1)
              + pltpu.roll(lo.astype(f32), 2 * N_SSM_HEADS, 1))
    return jnp.dot(packed.astype(bf16), e3, preferred_element_type=f32)


def _dot3_left(m01, v):
    hi, mid, lo = _split3(v)
    d = lambda a: jnp.dot(m01, a, preferred_element_type=f32)
    return d(hi) + d(mid) + d(lo)


def _in_proj_kernel(x_ref, g_ref, w_ref, wdt_ref, qk_ref, vt_ref, xz_ref, dt_ref, h_ref):
    j = pl.program_id(1)
    half = x_ref.shape[0] // 2

    @pl.when(j == 0)
    def _():
        for rows in (slice(0, half), slice(half, 2 * half)):
            hb = _rms(x_ref[rows], g_ref[...]).astype(bf16)
            h_ref[rows] = hb
            dt_ref[rows] = jnp.dot(hb, wdt_ref[...], preferred_element_type=f32)
            q = jnp.dot(hb, w_ref[...], preferred_element_type=f32)
            qk_ref[rows] = (q * (ATTN_HEAD_DIM ** -0.5 * LOG2E)).astype(bf16)

    @pl.when(j == 2)
    def _():
        tk = vt_ref.shape[-1]
        for u in range(vt_ref.shape[2]):
            v = jnp.dot(h_ref[u * tk:(u + 1) * tk], w_ref[...], preferred_element_type=f32)
            vt_ref[0, :, u] = v.T.astype(bf16).reshape(N_ATTN_HEADS, ATTN_V_DIM, tk)

    def project(out_ref):
        for rows in (slice(0, half), slice(half, 2 * half)):
            out_ref[rows] = jnp.dot(h_ref[rows], w_ref[...], preferred_element_type=f32).astype(bf16)

    @pl.when(j == 1)
    def _():
        project(qk_ref)

    @pl.when(j >= 3)
    def _():
        project(xz_ref)


def _in_proj(x2, gain, w_main, w_dt, bsz, seq, tm, tk):
    rows = x2.shape[0]
    nl = seq // tm
    nj = OFF_DT // ATTN_WIDTH
    n_qkv, z_tile = OFF_Z // ATTN_WIDTH, OFF_Z // ATTN_WIDTH
    w_tile = lambda i, j: (0, jnp.where(j < n_qkv, j, jnp.where(j < nj - 1, j + 1, z_tile)))
    return pl.pallas_call(
        _in_proj_kernel,
        grid=(rows // tm, nj),
        in_specs=[
            pl.BlockSpec((tm, D_MODEL), lambda i, j: (i, 0)),
            pl.BlockSpec((1, D_MODEL), lambda i, j: (0, 0)),
            pl.BlockSpec((D_MODEL, ATTN_WIDTH), w_tile),
            pl.BlockSpec((D_MODEL, LANES), lambda i, j: (0, 0)),
        ],
        out_specs=[
            pl.BlockSpec((tm, ATTN_WIDTH), lambda i, j: (i, jnp.minimum(j, 1))),
            pl.BlockSpec((1, N_ATTN_HEADS, tm // tk, ATTN_V_DIM, tk), lambda i, j: (i // nl, 0, i % nl, 0, 0)),
            pl.BlockSpec((tm, ATTN_WIDTH), lambda i, j: (i, jnp.clip(j - 3, 0, 2))),
            pl.BlockSpec((tm, LANES), lambda i, j: (i, 0)),
        ],
        out_shape=[
            jax.ShapeDtypeStruct((rows, 2 * ATTN_WIDTH), bf16),
            jax.ShapeDtypeStruct((bsz, N_ATTN_HEADS, seq // tk, ATTN_V_DIM, tk), bf16),
            jax.ShapeDtypeStruct((rows, CONV_CH + SSM_WIDTH), bf16),
            jax.ShapeDtypeStruct((rows, LANES), f32),
        ],
        scratch_shapes=[pltpu.VMEM((tm, D_MODEL), bf16)],
        compiler_params=_cparams(2),
        name="in_proj",
    )(x2, gain, w_main, w_dt)


def _bias_block(tab_ref, h, delta):
    kk = lax.broadcasted_iota(jnp.int32, (MAX_DISTANCE, MAX_DISTANCE), 0)
    qq = lax.broadcasted_iota(jnp.int32, (MAX_DISTANCE, MAX_DISTANCE), 1)
    rel = delta * MAX_DISTANCE + kk - qq
    nb = NUM_BUCKETS // 2
    max_exact = nb // 2
    ret = jnp.where(rel > 0, nb, 0)
    n = jnp.abs(rel)
    nf = jnp.maximum(n, 1).astype(f32)
    large = max_exact + (jnp.log(nf / max_exact) / math.log(MAX_DISTANCE / max_exact)
                         * (nb - max_exact)).astype(jnp.int32)
    large = jnp.minimum(large, nb - 1)
    bucket = ret + jnp.where(n < max_exact, n, large)
    out = jnp.zeros(rel.shape, f32)
    for b in range(NUM_BUCKETS):
        out = jnp.where(bucket == b, tab_ref[b, h] * LOG2E, out)
    return out


def _bias_kernel(tab_ref, o_ref, *, tq, tk):
    h = pl.program_id(0)
    blk = MAX_DISTANCE
    side = {-2: jnp.full((blk, blk), tab_ref[NUM_BUCKETS // 2 - 1, h] * LOG2E, f32),
            2: jnp.full((blk, blk), tab_ref[NUM_BUCKETS - 1, h] * LOG2E, f32)}
    band = {delta: _bias_block(tab_ref, h, delta) for delta in (-1, 0, 1)}
    for d in range(tq // tk + 2):
        for a in range(tk // blk):
            for b in range(tq // blk):
                delta = (d - 1) * (tk // blk) + a - b
                val = band[delta] if abs(delta) <= 1 else side[2 if delta > 0 else -2]
                o_ref[0, d, a * blk:(a + 1) * blk, b * blk:(b + 1) * blk] = val


def _bias_tiles(rel_bias, tq, tk):
    n_near = tq // tk + 2
    return pl.pallas_call(
        functools.partial(_bias_kernel, tq=tq, tk=tk),
        grid=(N_ATTN_HEADS,),
        in_specs=[pl.BlockSpec(memory_space=pltpu.SMEM)],
        out_specs=pl.BlockSpec((1, n_near, tk, tq), lambda h: (h, 0, 0, 0)),
        out_shape=jax.ShapeDtypeStruct((N_ATTN_HEADS, n_near, tk, tq), f32),
        compiler_params=_cparams(1),
        name="bias_tiles",
    )(rel_bias)


def _attn_kernel(tab_ref, lam_ref, q_ref, k_ref, vt_ref, bias_ref, w_ref, o_ref,
                 qcat_ref, s_ref, p_ref, m_ref, acc_ref, *, tq, tk, nk, out_scale):
    h = pl.program_id(1)
    qi = pl.program_id(2)
    r = tq // tk
    c_left = tab_ref[NUM_BUCKETS // 2 - 1, h] * LOG2E
    c_right = tab_ref[NUM_BUCKETS - 1, h] * LOG2E
    lam = lam_ref[0]

    qt = q_ref[0].astype(f32).T
    row = lax.broadcasted_iota(jnp.int32, qt.shape, 0)
    qcat_ref[...] = jnp.concatenate([jnp.where(row < ATTN_HEAD_DIM, qt, 0.0),
                                     jnp.where(row >= ATTN_HEAD_DIM, qt, 0.0)], axis=1).astype(bf16)
    ones = jnp.ones((SUBLANES, tk), bf16)
    near_lo = jnp.maximum(qi * r - 1, 0)
    near_hi = jnp.minimum(qi * r + r + 1, nk)
    n_far = nk - (near_hi - near_lo)

    def logits(kt):
        return jnp.dot(k_ref[0, kt], qcat_ref[...], preferred_element_type=f32)

    def pv(kt, p):
        vext = jnp.concatenate([vt_ref[0, 0, kt], ones], axis=0)
        return jnp.dot(vext, p, preferred_element_type=f32)

    def tile_kt(t):
        far = t + jnp.where(t >= near_lo, near_hi - near_lo, 0)
        return jnp.where(t < n_far, far, near_lo + t - n_far)

    def tile_c(t):
        return jnp.where(t < n_far, jnp.where(t >= near_lo, c_right, c_left), 0.0)

    def stage_a(t, near):
        kt = tile_kt(t)
        s = logits(kt)
        if near:
            b = bias_ref[0, kt - qi * r + 1]
            s = s + jnp.concatenate([b, b], axis=1)
        s_ref[...] = s
        return jnp.max(s, axis=0, keepdims=True)

    def stage_b(t, smax):
        c = tile_c(t)
        m = m_ref[...]
        m_new = jnp.maximum(m, smax + c)
        p_ref[...] = jnp.exp2(s_ref[...] - (m_new - c)).astype(bf16)
        m_ref[...] = m_new
        return jnp.exp2(m - m_new)

    def stage_c(t, alpha):
        acc_ref[...] = acc_ref[...] * alpha + pv(tile_kt(t), p_ref[...])

    m_ref[...] = jnp.full(m_ref.shape, NEG, f32)
    acc_ref[...] = jnp.zeros(acc_ref.shape, f32)

    smax = stage_a(0, False)
    alpha = stage_b(0, smax)
    smax = stage_a(1, False)

    def step(t, carry, near):
        alpha, smax = carry
        stage_c(t - 2, alpha)
        alpha = stage_b(t - 1, smax)
        return alpha, stage_a(t, near)

    def run_steps(lo, hi, near, carry):
        first = lo + jnp.bitwise_and(hi - lo, TILE_UNROLL - 1)
        carry = lax.fori_loop(lo, first, lambda t, cr: step(t, cr, near), carry)

        def group(j, cr):
            for u in range(TILE_UNROLL):
                cr = step(first + TILE_UNROLL * j + u, cr, near)
            return cr

        n_groups = lax.shift_right_logical(hi - first, TILE_UNROLL.bit_length() - 1)
        return lax.fori_loop(0, n_groups, group, carry)

    carry = run_steps(2, n_far, False, (alpha, smax))
    alpha, smax = run_steps(n_far, nk, True, carry)
    stage_c(nk - 2, alpha)
    alpha = stage_b(nk - 1, smax)
    stage_c(nk - 1, alpha)

    acc = acc_ref[...]
    o = acc[:ATTN_V_DIM] * (1.0 / acc[ATTN_V_DIM:ATTN_V_DIM + 1])
    diff = (o[:, :tq] - lam * o[:, tq:]).T
    o_ref[0] = (_rms(diff, w_ref[...]) * out_scale).astype(bf16)


def _attention(qk3, vt, bias, rel_bias, lam, subln_w, lambda_init, tq, tk):
    bsz, seq = qk3.shape[0], qk3.shape[1]
    nk = seq // tk
    assert nk >= tq // tk + 4, "the far-tile pipeline needs at least two far key tiles per query tile"
    qk4 = qk3.reshape(bsz, nk, tk, 2 * ATTN_WIDTH)
    kern = functools.partial(_attn_kernel, tq=tq, tk=tk, nk=nk, out_scale=1.0 - lambda_init)
    return pl.pallas_call(
        kern,
        grid=(bsz, N_ATTN_HEADS, seq // tq),
        in_specs=[
            pl.BlockSpec(memory_space=pltpu.SMEM),
            pl.BlockSpec(memory_space=pltpu.SMEM),
            pl.BlockSpec((1, tq, ATTN_V_DIM), lambda b, h, i: (b, i, h)),
            pl.BlockSpec((1, nk, tk, ATTN_V_DIM), lambda b, h, i: (b, 0, 0, N_ATTN_HEADS + h)),
            pl.BlockSpec((1, 1, nk, ATTN_V_DIM, tk), lambda b, h, i: (b, h, 0, 0, 0)),
            pl.BlockSpec((1, tq // tk + 2, tk, tq), lambda b, h, i: (h, 0, 0, 0)),
            pl.BlockSpec((1, ATTN_V_DIM), lambda b, h, i: (0, 0)),
        ],
        out_specs=pl.BlockSpec((1, tq, ATTN_V_DIM), lambda b, h, i: (b, i, h)),
        out_shape=jax.ShapeDtypeStruct((bsz, seq, ATTN_WIDTH), bf16),
        scratch_shapes=[
            pltpu.VMEM((ATTN_V_DIM, 2 * tq), bf16),
            pltpu.VMEM((tk, 2 * tq), f32),
            pltpu.VMEM((tk, 2 * tq), bf16),
            pltpu.VMEM((1, 2 * tq), f32),
            pltpu.VMEM((ATTN_V_DIM + SUBLANES, 2 * tq), f32),
        ],
        compiler_params=_cparams(3),
        name="attention",
    )(rel_bias, lam, qk3, qk4, vt, bias, subln_w)


def _softplus(v):
    return jnp.maximum(v, 0.0) + jnp.log1p(jnp.exp(-jnp.abs(v)))


def _ssd_decay_terms(dt_raw, dtb_ref, alog_ref, tril_ref):
    dtv = _softplus(dt_raw + dtb_ref[...])
    lane = lax.broadcasted_iota(jnp.int32, (1, LANES), 1)
    a_row = jnp.where(lane < 2 * N_SSM_HEADS, -jnp.exp(alog_ref[...]), 0.0)
    a = dtv * a_row
    cs = _dot3_left(tril_ref[...], a)
    return dtv, a, cs


def _ssd_fwd_kernel(cur_ref, prev_ref, next_ref, dt_ref, cw_ref, cb_ref, dtb_ref, alog_ref, dskip_ref,
                    tril_ref, triu_ref, ef_ref, shift_ref, y_ref, act_ref, state_ref):
    c = pl.program_id(1)
    nc = pl.num_programs(1)

    @pl.when(c == 0)
    def _():
        state_ref[...] = jnp.zeros_like(state_ref)

    pmask = (c > 0).astype(f32)
    nmask = (c < nc - 1).astype(f32)
    ext = jnp.concatenate([prev_ref[0] * pmask.astype(bf16), cur_ref[0], next_ref[0] * nmask.astype(bf16)],
                          axis=0)
    taps = jnp.dot(shift_ref[...], ext, preferred_element_type=f32)
    mid = D_CONV // 2
    conv = cur_ref[0].astype(f32) * cw_ref[mid:mid + 1, :] + cb_ref[...]
    for i, j in enumerate(_SIDE_TAPS):
        conv = conv + taps[i * CHUNK:(i + 1) * CHUNK, :] * cw_ref[j:j + 1, :]
    act = conv * (1.0 / (1.0 + jnp.exp(-conv)))
    actb = act.astype(bf16)
    act_ref[0] = actb

    dtv, a, cs = _ssd_decay_terms(dt_ref[0], dtb_ref, alog_ref, tril_ref)
    a_t = a.T
    dt_t = dtv.T
    cs_t = _dot3(a_t, triu_ref[...])
    ecs = cs - a
    ecs_t = cs_t - a_t

    tot = cs[CHUNK - 1:CHUNK, :]
    scale_off = _expand_heads(jnp.exp(cs), ef_ref[...], 0)
    scale_in = _expand_heads(dtv * jnp.exp(tot - cs), ef_ref[...], 0)
    carry_dec = scale_off[CHUNK - 1:CHUNK, :]

    log2_dt_t = jnp.log(dt_t) * LOG2E
    col_f, row_f = cs * LOG2E, cs_t * LOG2E - log2_dt_t
    col_b, row_b = ecs * LOG2E, ecs_t * LOG2E + log2_dt_t
    li = lax.broadcasted_iota(jnp.int32, (CHUNK, CHUNK), 0)
    si = lax.broadcasted_iota(jnp.int32, (CHUNK, CHUNK), 1)
    lower = li >= si
    upper = si >= li

    for g in range(N_SSM_GROUPS):
        gs = slice(g * GROUP_W, (g + 1) * GROUP_W)
        bg = actb[:, SSM_WIDTH + g * D_STATE: SSM_WIDTH + (g + 1) * D_STATE]
        cg = actb[:, SSM_WIDTH + N_SSM_GROUPS * D_STATE + g * D_STATE:
                  SSM_WIDTH + N_SSM_GROUPS * D_STATE + (g + 1) * D_STATE]
        cbm = lax.dot_general(cg, bg, (((1,), (1,)), ((), ())), preferred_element_type=f32)
        ws, xs_blocks = [], []
        for hh in range(SSM_HPG):
            hd = g * SSM_HPG + hh
            hb = N_SSM_HEADS + hd
            dec_f = jnp.exp2(jnp.where(lower, col_f[:, hd:hd + 1] - row_f[hd:hd + 1, :], NEG))
            dec_b = jnp.exp2(jnp.where(upper, row_b[hb:hb + 1, :] - col_b[:, hb:hb + 1], NEG))
            ws.append((cbm * (dec_f + dec_b)).astype(bf16))
            lane = lax.broadcasted_iota(jnp.int32, (CHUNK, GROUP_W), 1)
            own = (lane >= hh * SSM_HEAD_DIM) & (lane < (hh + 1) * SSM_HEAD_DIM)
            xs_blocks.append(jnp.where(own, actb[:, gs], jnp.zeros((), bf16)))
        xg = act[:, gs]
        wcat = jnp.concatenate(ws, axis=1)
        xblk = jnp.concatenate(xs_blocks, axis=0)
        y = jnp.dot(wcat, xblk, preferred_element_type=f32)
        y = y + jnp.dot(cg, state_ref[g].astype(bf16), preferred_element_type=f32) * scale_off[:, gs]
        y_ref[0, :, gs] = y + xg * dskip_ref[:, gs]
        xs = (xg * scale_in[:, gs]).astype(bf16)
        new = jnp.dot(bg.astype(f32).T.astype(bf16), xs, preferred_element_type=f32)
        state_ref[g] = state_ref[g] * carry_dec[:, gs] + new


def _ssd_bwd_kernel(act_ref, dt_ref, y_ref, z_ref, dtb_ref, alog_ref, nw_ref, tril_ref, eb_ref,
                    o_ref, state_ref):
    c = pl.program_id(1)

    @pl.when(c == 0)
    def _():
        state_ref[...] = jnp.zeros_like(state_ref)

    for sub in reversed(range(act_ref.shape[1] // CHUNK)):
        rows = slice(sub * CHUNK, (sub + 1) * CHUNK)
        actb = act_ref[0, rows]
        dtv, a, cs = _ssd_decay_terms(dt_ref[0, rows], dtb_ref, alog_ref, tril_ref)
        ecs = cs - a
        tot = cs[CHUNK - 1:CHUNK, :]
        scale_off = _expand_heads(jnp.exp(tot - ecs), eb_ref[...], N_SSM_HEADS)
        scale_in = _expand_heads(dtv * jnp.exp(ecs), eb_ref[...], N_SSM_HEADS)
        carry_dec = scale_off[0:1, :]

        zf = z_ref[0, rows].astype(f32)
        gate = zf * (1.0 / (1.0 + jnp.exp(-zf)))
        for g in range(N_SSM_GROUPS):
            gs = slice(g * GROUP_W, (g + 1) * GROUP_W)
            bg = actb[:, SSM_WIDTH + g * D_STATE: SSM_WIDTH + (g + 1) * D_STATE]
            cg = actb[:, SSM_WIDTH + N_SSM_GROUPS * D_STATE + g * D_STATE:
                      SSM_WIDTH + N_SSM_GROUPS * D_STATE + (g + 1) * D_STATE]
            y = y_ref[0, rows, gs] + (jnp.dot(cg, state_ref[g].astype(bf16), preferred_element_type=f32)
                                      * scale_off[:, gs])
            xs = (actb[:, gs].astype(f32) * scale_in[:, gs]).astype(bf16)
            new = jnp.dot(bg.astype(f32).T.astype(bf16), xs, preferred_element_type=f32)
            state_ref[g] = state_ref[g] * carry_dec[:, gs] + new
            o_ref[0, rows, gs] = _rms(y * gate[:, gs], nw_ref[:, gs]).astype(bf16)


def _const_spec(shape):
    return pl.BlockSpec(shape, lambda b, c: (0,) * len(shape))


def _ssd_fwd(xz3, dt3, conv_w, conv_b, dt_bias, a_log, d_skip, consts):
    bsz, seq = xz3.shape[0], xz3.shape[1]
    nc = seq // CHUNK
    hb = CHUNK // HALO
    nh = seq // HALO
    tril, triu, ef, _, shift = consts
    const = _const_spec
    return pl.pallas_call(
        _ssd_fwd_kernel,
        grid=(bsz, nc),
        in_specs=[
            pl.BlockSpec((1, CHUNK, CONV_CH), lambda b, c: (b, c, 0)),
            pl.BlockSpec((1, HALO, CONV_CH), lambda b, c: (b, jnp.maximum(c * hb - 1, 0), 0)),
            pl.BlockSpec((1, HALO, CONV_CH), lambda b, c: (b, jnp.minimum((c + 1) * hb, nh - 1), 0)),
            pl.BlockSpec((1, CHUNK, LANES), lambda b, c: (b, c, 0)),
            const((SUBLANES, CONV_CH)), const((1, CONV_CH)), const((1, LANES)), const((1, LANES)),
            const((1, SSM_WIDTH)), const((CHUNK, CHUNK)), const((CHUNK, CHUNK)), const((LANES, SSM_WIDTH)),
            const(((D_CONV - 1) * CHUNK, CHUNK + 2 * HALO)),
        ],
        out_specs=[
            pl.BlockSpec((1, CHUNK, SSM_WIDTH), lambda b, c: (b, c, 0)),
            pl.BlockSpec((1, CHUNK, CONV_CH), lambda b, c: (b, c, 0)),
        ],
        out_shape=[
            jax.ShapeDtypeStruct((bsz, seq, SSM_WIDTH), f32),
            jax.ShapeDtypeStruct((bsz, seq, CONV_CH), bf16),
        ],
        scratch_shapes=[pltpu.VMEM((N_SSM_GROUPS, D_STATE, GROUP_W), f32)],
        compiler_params=_cparams(2),
        name="ssd_fwd",
    )(xz3, xz3, xz3, dt3, conv_w, conv_b, dt_bias, a_log, d_skip, tril, triu, ef, shift)


def _ssd_bwd(act, dt3, y_part, xz3, dt_bias, a_log, norm_w, consts):
    bsz, seq = xz3.shape[0], xz3.shape[1]
    per_step = SSD_BWD_CHUNKS if (seq // CHUNK) % SSD_BWD_CHUNKS == 0 else 1
    blk = per_step * CHUNK
    nc = seq // blk
    tril, _, _, eb, _ = consts
    const = _const_spec
    rev = lambda b, c: (b, nc - 1 - c, 0)
    return pl.pallas_call(
        _ssd_bwd_kernel,
        grid=(bsz, nc),
        in_specs=[
            pl.BlockSpec((1, blk, CONV_CH), rev),
            pl.BlockSpec((1, blk, LANES), rev),
            pl.BlockSpec((1, blk, SSM_WIDTH), rev),
            pl.BlockSpec((1, blk, SSM_WIDTH), lambda b, c: (b, nc - 1 - c, CONV_CH // SSM_WIDTH)),
            const((1, LANES)), const((1, LANES)), const((1, SSM_WIDTH)),
            const((CHUNK, CHUNK)), const((LANES, SSM_WIDTH)),
        ],
        out_specs=pl.BlockSpec((1, blk, SSM_WIDTH), rev),
        out_shape=jax.ShapeDtypeStruct((bsz, seq, SSM_WIDTH), bf16),
        scratch_shapes=[pltpu.VMEM((N_SSM_GROUPS, D_STATE, GROUP_W), f32)],
        compiler_params=_cparams(2),
        name="ssd_bwd",
    )(act, dt3, y_part, xz3, dt_bias, a_log, norm_w, tril, eb)


def _ssd(xz3, dt3, conv_w, conv_b, dt_bias, a_log, d_skip, norm_w, consts):
    y_part, act = _ssd_fwd(xz3, dt3, conv_w, conv_b, dt_bias, a_log, d_skip, consts)
    return _ssd_bwd(act, dt3, y_part, xz3, dt_bias, a_log, norm_w, consts)


def _ssd_consts():
    idx = np.arange(CHUNK)
    tril = (idx[None, :] <= idx[:, None]).astype(np.float32)
    head = np.arange(SSM_WIDTH) // SSM_HEAD_DIM
    rows = np.arange(LANES)
    piece = lambda first: (rows >= first) & (rows < first + 3 * N_SSM_HEADS)
    ef = (piece(0)[:, None] & (rows[:, None] % N_SSM_HEADS == head[None, :])).astype(np.float32)
    eb = (piece(N_SSM_HEADS)[:, None] & (rows[:, None] % N_SSM_HEADS == head[None, :])).astype(np.float32)
    out_row = np.arange((D_CONV - 1) * CHUNK)
    tap = np.asarray(_SIDE_TAPS)[out_row // CHUNK]
    src = HALO + out_row % CHUNK + tap - D_CONV // 2
    shift = (src[:, None] == np.arange(CHUNK + 2 * HALO)[None, :]).astype(np.float32)
    return tuple(jnp.asarray(m, bf16) for m in (tril, tril.T, ef, eb, shift))


def _out_proj_kernel(a_ref, s_ref, w_ref, x_ref, g_ref, o_ref):
    mix = jnp.dot(a_ref[...], w_ref[:ATTN_WIDTH, :], preferred_element_type=f32)
    mix = mix + jnp.dot(s_ref[...], w_ref[ATTN_WIDTH:, :], preferred_element_type=f32)
    o_ref[...] = x_ref[...] + _rms(mix, g_ref[...])


def _out_proj(a2, s2, w_out, x2, gain, tm):
    rows = x2.shape[0]
    return pl.pallas_call(
        _out_proj_kernel,
        grid=(rows // tm,),
        in_specs=[
            pl.BlockSpec((tm, ATTN_WIDTH), lambda i: (i, 0)),
            pl.BlockSpec((tm, SSM_WIDTH), lambda i: (i, 0)),
            pl.BlockSpec((D_MODEL, D_MODEL), lambda i: (0, 0)),
            pl.BlockSpec((tm, D_MODEL), lambda i: (i, 0)),
            pl.BlockSpec((1, D_MODEL), lambda i: (0, 0)),
        ],
        out_specs=pl.BlockSpec((tm, D_MODEL), lambda i: (i, 0)),
        out_shape=jax.ShapeDtypeStruct((rows, D_MODEL), f32),
        compiler_params=_cparams(1),
        name="out_proj",
    )(a2, s2, w_out, x2, gain)


def _mlp_kernel(x_ref, gpre_ref, wup_ref, wdn_ref, gpost_ref, o_ref, h_ref):
    j = pl.program_id(1)
    last = pl.num_programs(1) - 1
    part = x_ref.shape[0] // MLP_EDGE_PARTS
    halves = tuple(slice(u * part, (u + 1) * part) for u in range(MLP_EDGE_PARTS))

    def ff(h):
        u = jnp.dot(h, wup_ref[...], preferred_element_type=f32)
        u = jnp.square(jnp.maximum(u, 0.0)).astype(bf16)
        return jnp.dot(u, wdn_ref[...], preferred_element_type=f32)

    @pl.when(j == 0)
    def _():
        for rows in halves:
            h = _rms(x_ref[rows], gpre_ref[...]).astype(bf16)
            h_ref[rows] = h
            o_ref[rows] = ff(h)

    @pl.when((j > 0) & (j < last))
    def _():
        o_ref[...] += ff(h_ref[...])

    @pl.when(j == last)
    def _():
        for rows in halves:
            y = o_ref[rows] + ff(h_ref[rows])
            o_ref[rows] = x_ref[rows] + _rms(y, gpost_ref[...])


def _mlp(x2, g_pre, w_up, w_down, g_post, tm, tf):
    rows = x2.shape[0]
    assert D_FF // tf >= 2, "the first and the last d_ff step are distinct code paths"
    return pl.pallas_call(
        _mlp_kernel,
        grid=(rows // tm, D_FF // tf),
        in_specs=[
            pl.BlockSpec((tm, D_MODEL), lambda i, j: (i, 0)),
            pl.BlockSpec((1, D_MODEL), lambda i, j: (0, 0)),
            pl.BlockSpec((D_MODEL, tf), lambda i, j: (0, j)),
            pl.BlockSpec((tf, D_MODEL), lambda i, j: (j, 0)),
            pl.BlockSpec((1, D_MODEL), lambda i, j: (0, 0)),
        ],
        out_specs=pl.BlockSpec((tm, D_MODEL), lambda i, j: (i, 0)),
        out_shape=jax.ShapeDtypeStruct((rows, D_MODEL), f32),
        scratch_shapes=[pltpu.VMEM((tm, D_MODEL), bf16)],
        compiler_params=_cparams(2),
        name="mlp",
    )(x2, g_pre, w_up, w_down, g_post)


def _attn_tiles(seq):
    for tq, tk in ((1024, 512), (512, 512), (256, 256)):
        if seq % tq == 0 and seq // tk >= tq // tk + 4:
            return tq, tk
    raise ValueError(f"sequence length {seq} is too short for the attention tiling")


def _trunk(x, layers, rel_bias, bias, consts, tq, tk):
    bsz, seq = x.shape[0], x.shape[1]
    rows = bsz * seq
    tm = tk
    tm_wide = ROW_TILE if seq % ROW_TILE == 0 and ROW_TILE % tk == 0 else tm
    x2 = x.reshape(rows, D_MODEL)
    for lp in layers:
        qk, vt, xz, dt = _in_proj(x2, lp["pre_norm_mix"], lp["w_main"], lp["w_dt"], bsz, seq, tm_wide, tk)
        a_out = _attention(qk.reshape(bsz, seq, 2 * ATTN_WIDTH), vt, bias, rel_bias, lp["lam"], lp["attn_norm"],
                           lp["lambda_init"], tq, tk)
        s_out = _ssd(xz.reshape(bsz, seq, CONV_CH + SSM_WIDTH), dt.reshape(bsz, seq, LANES),
                     lp["conv_w"], lp["conv_b"], lp["dt_bias"], lp["a_log"], lp["d_skip"], lp["ssm_norm"],
                     consts)
        x2 = _out_proj(a_out.reshape(rows, ATTN_WIDTH), s_out.reshape(rows, SSM_WIDTH), lp["w_out"], x2,
                       lp["post_norm_mix"], tm)
        x2 = _mlp(x2, lp["pre_norm_mlp"], lp["w_up"], lp["w_down"], lp["post_norm_mlp"], tm_wide, FF_TILE)
    return x2.reshape(bsz, seq, D_MODEL)


def _pad_lanes(v, width=LANES):
    return jnp.pad(v, (0, width - v.shape[0]))[None, :].astype(f32)


def kernel(x_prompt, x_sample, rel_bias, pre_norm_mix, w_in, lambda_q1, lambda_k1, lambda_q2, lambda_k2,
           attn_norm, conv_w, conv_b, dt_bias_fwd, dt_bias_bwd, a_log_fwd, a_log_bwd, d_skip, ssm_norm,
           w_out, post_norm_mix, pre_norm_mlp, w_up, w_down, post_norm_mlp):
    depth = w_in.shape[0]
    row = lambda v: v[None, :].astype(f32)
    layers = []
    for i in range(depth):
        lambda_init = 0.8 - 0.6 * math.exp(-0.3 * i)
        lam = (jnp.exp(jnp.sum(lambda_q1[i].astype(f32) * lambda_k1[i].astype(f32)))
               - jnp.exp(jnp.sum(lambda_q2[i].astype(f32) * lambda_k2[i].astype(f32))) + lambda_init)
        wi = w_in[i]
        w_main = wi.astype(bf16)
        w_dt = jnp.pad(wi[:, OFF_DT:], ((0, 0), (0, LANES - 2 * N_SSM_HEADS))).astype(bf16)
        layers.append(dict(
            lambda_init=lambda_init,
            lam=lam.reshape(1).astype(f32),
            pre_norm_mix=row(pre_norm_mix[i]), w_main=w_main, w_dt=w_dt,
            attn_norm=row(attn_norm[i]),
            conv_w=jnp.pad(conv_w[i].astype(f32), ((0, SUBLANES - D_CONV), (0, 0))), conv_b=row(conv_b[i]),
            dt_bias=_pad_lanes(jnp.concatenate([dt_bias_fwd[i], dt_bias_bwd[i]])),
            a_log=_pad_lanes(jnp.concatenate([a_log_fwd[i], a_log_bwd[i]])),
            d_skip=row(jnp.repeat(d_skip[i], SSM_HEAD_DIM)), ssm_norm=row(ssm_norm[i]),
            w_out=w_out[i].astype(bf16), post_norm_mix=row(post_norm_mix[i]),
            pre_norm_mlp=row(pre_norm_mlp[i]), w_up=w_up[i].astype(bf16), w_down=w_down[i].astype(bf16),
            post_norm_mlp=row(post_norm_mlp[i]),
        ))
    consts = _ssd_consts()
    rel = rel_bias.astype(f32)
    outs = []
    biases = {}
    for x in (x_prompt, x_sample):
        tq, tk = _attn_tiles(x.shape[1])
        if (tq, tk) not in biases:
            biases[(tq, tk)] = _bias_tiles(rel, tq, tk)
        outs.append(_trunk(x, layers, rel, biases[(tq, tk)], consts, tq, tk))
    return tuple(outs)
```
